```python
import math
import jax, jax.numpy as jnp
from jax import lax
import numpy as np

D_MODEL = 1024
BATCH = 8
SEQ = 4096
DEPTH = 1
DEC_BATCH = 32
DEC_SEQ = 8
PAST_LEN = 16384
PAGE_SIZE = 128

N_MEM = 256
HEAD_DIM = 64
N_HEADS_MOBA = 8
N_HEADS_SB = 8
N_HEADS_MEM = 4
HEAD_DIM_MEM = 128
W_MOBA = N_HEADS_MOBA * HEAD_DIM
W_SB = N_HEADS_SB * HEAD_DIM
W_MEM = N_HEADS_MEM * HEAD_DIM_MEM
N_BRANCH = 3
D_IN = 3 * W_MOBA + 3 * W_SB + W_MEM + N_BRANCH * D_MODEL
MOBA_BLOCK = 256
MOBA_TOPK = 3
Q_BLOCK = 128
ROT_DIM = HEAD_DIM // 4
ROPE_THETA = 500000.0
D_FF = 2816
EPS = 1e-6

kernel_name = "hybrid_moba_stickbreaking_memory_decoder_step"


def rmsnorm(x, g):
    xf = x.astype(jnp.float32)
    y = xf * lax.rsqrt(jnp.mean(xf * xf, axis=-1, keepdims=True) + EPS)
    return (y * g.astype(jnp.float32)).astype(x.dtype)


def swiglu(h, w_gu, w_down):
    g, u = jnp.split(h @ w_gu, 2, axis=-1)
    return (jax.nn.silu(g) * u) @ w_down


def rope_partial(x, pos):
    half = ROT_DIM // 2
    inv = ROPE_THETA ** (-(jnp.arange(half, dtype=jnp.float32) * 2.0) / ROT_DIM)
    ang = pos.astype(jnp.float32)[:, None] * inv[None, :]
    cos = jnp.cos(ang)[:, None, :]
    sin = jnp.sin(ang)[:, None, :]
    xf = x.astype(jnp.float32)
    x1 = xf[..., :half]
    x2 = xf[..., half:ROT_DIM]
    out = jnp.concatenate([x1 * cos - x2 * sin, x1 * sin + x2 * cos, xf[..., ROT_DIM:]], axis=-1)
    return out.astype(x.dtype)


def split_proj(z):
    B, S, _ = z.shape
    sizes = (W_MOBA,) * 3 + (W_SB,) * 3 + (W_MEM, N_BRANCH * D_MODEL)
    parts = jnp.split(z, [int(c) for c in np.cumsum(sizes)[:-1]], axis=-1)
    q_a, k_a, v_a = [p.reshape(B, S, N_HEADS_MOBA, HEAD_DIM) for p in parts[0:3]]
    q_b, k_b, v_b = [p.reshape(B, S, N_HEADS_SB, HEAD_DIM) for p in parts[3:6]]
    q_m = parts[6].reshape(B, S, N_HEADS_MEM, HEAD_DIM_MEM)
    gates = parts[7].reshape(B, S, N_BRANCH, D_MODEL)
    return q_a, k_a, v_a, q_b, k_b, v_b, q_m, gates


def moba_softmax(q, k_sel, v_sel, sel_valid, k_own, v_own, own_mask):
    B, Q, H, d = q.shape
    qf = q.astype(jnp.float32) * (d ** -0.5)
    lo = jnp.einsum("bqhd,bhld->bqhl", qf, k_own.astype(jnp.float32))
    lo = jnp.where(own_mask[None, :, None, :], lo, -jnp.inf)
    n_own = lo.shape[-1]
    if k_sel is None:
        w = jax.nn.softmax(lo, axis=-1)
        out = jnp.einsum("bqhl,bhld->bqhd", w, v_own.astype(jnp.float32))
        return out.astype(q.dtype)
    ls = jnp.einsum("bqhd,bqhnld->bqhnl", qf, k_sel.astype(jnp.float32))
    ls = jnp.where(sel_valid[..., None], ls, -jnp.inf)
    sel_shape = ls.shape
    n_s = sel_shape[3] * sel_shape[4]
    w = jax.nn.softmax(jnp.concatenate([ls.reshape(B, Q, H, n_s), lo], axis=-1), axis=-1)
    out = jnp.einsum("bqhl,bhld->bqhd", w[..., n_s:], v_own.astype(jnp.float32))
    out = out + jnp.einsum("bqhnl,bqhnld->bqhd", w[..., :n_s].reshape(sel_shape), v_sel.astype(jnp.float32))
    return out.astype(q.dtype)


def moba_prompt(q, k, v):
    B, S, H, d = q.shape
    nb = -(-S // MOBA_BLOCK)
    pad = ((0, 0), (0, nb * MOBA_BLOCK - S), (0, 0), (0, 0))
    kb = jnp.pad(k, pad).reshape(B, nb, MOBA_BLOCK, H, d).transpose(0, 3, 1, 2, 4)
    vb = jnp.pad(v, pad).reshape(B, nb, MOBA_BLOCK, H, d).transpose(0, 3, 1, 2, 4)
    kbar = kb.astype(jnp.float32).mean(axis=3)
    n_sel = min(MOBA_TOPK, nb)
    b_i = jnp.arange(B)[:, None, None, None]
    h_i = jnp.arange(H)[None, None, :, None]
    blk = jnp.arange(nb)

    def one(qi):
        qs = qi * Q_BLOCK
        qq = lax.dynamic_slice_in_dim(q, qs, Q_BLOCK, axis=1)
        qpos = qs + jnp.arange(Q_BLOCK)
        ob = qs // MOBA_BLOCK
        sc = jnp.einsum("bqhd,bhnd->bqhn", qq.astype(jnp.float32), kbar)
        sc = jnp.where(blk < ob, sc, -jnp.inf)
        _, idx = lax.top_k(sc, n_sel)
        valid = idx < ob
        k_sel = kb[b_i, h_i, idx]
        v_sel = vb[b_i, h_i, idx]
        k_own = lax.dynamic_index_in_dim(kb, ob, axis=2, keepdims=False)
        v_own = lax.dynamic_index_in_dim(vb, ob, axis=2, keepdims=False)
        own_pos = ob * MOBA_BLOCK + jnp.arange(MOBA_BLOCK)
        own_mask = own_pos[None, :] <= qpos[:, None]
        return moba_softmax(qq, k_sel, v_sel, valid, k_own, v_own, own_mask)

    out = lax.map(one, jnp.arange(S // Q_BLOCK))
    return out.transpose(1, 0, 2, 3, 4).reshape(B, S, H, d)


def moba_decode(q, k_new, v_new, cache_k, cache_v, page_table):
    DB, T, H, d = q.shape
    n_pages = page_table.shape[1]
    past = n_pages * PAGE_SIZE
    ppb = MOBA_BLOCK // PAGE_SIZE
    nb_past = past // MOBA_BLOCK
    own_past = past % MOBA_BLOCK
    if own_past > 0:
        own_pages = page_table[:, nb_past * ppb:]
        k_op = cache_k[own_pages].reshape(DB, own_past, H, d)
        v_op = cache_v[own_pages].reshape(DB, own_past, H, d)
        k_own = jnp.concatenate([k_op, k_new], axis=1)
        v_own = jnp.concatenate([v_op, v_new], axis=1)
    else:
        k_own, v_own = k_new, v_new
    k_own = k_own.transpose(0, 2, 1, 3)
    v_own = v_own.transpose(0, 2, 1, 3)
    tt = jnp.arange(T)
    own_mask = jnp.concatenate([jnp.ones((T, own_past), dtype=bool), tt[None, :] <= tt[:, None]], axis=1)
    if nb_past == 0:
        return moba_softmax(q, None, None, None, k_own, v_own, own_mask)
    full = page_table[:, : nb_past * ppb]
    psum = lax.map(lambda p: cache_k[p].astype(jnp.float32).sum(axis=1), full.T)
    kbar = psum.reshape(nb_past, ppb, DB, H, d).sum(axis=1) / MOBA_BLOCK
    sc = jnp.einsum("bthd,nbhd->bthn", q.astype(jnp.float32), kbar)
    n_sel = min(MOBA_TOPK, nb_past)
    _, idx = lax.top_k(sc, n_sel)
    valid = jnp.ones(idx.shape, dtype=bool)
    b_i = jnp.arange(DB)[:, None, None, None, None]
    phys = page_table[b_i, idx[..., None] * ppb + jnp.arange(ppb)]
    h_g = jnp.arange(H)[None, None, :, None, None, None]
    rows = jnp.arange(PAGE_SIZE)
    k_sel = cache_k[phys[..., None], rows, h_g].reshape(DB, T, H, n_sel, MOBA_BLOCK, d)
    v_sel = cache_v[phys[..., None], rows, h_g].reshape(DB, T, H, n_sel, MOBA_BLOCK, d)
    return moba_softmax(q, k_sel, v_sel, valid, k_own, v_own, own_mask)


def sb_prompt(q, k, v):
    B, S, H, d = q.shape
    kf = k.astype(jnp.float32)
    vf = v.astype(jnp.float32)
    kpos = jnp.arange(S)
    scale = d ** -0.5

    def one(qi):
        qs = qi * Q_BLOCK
        qq = lax.dynamic_slice_in_dim(q, qs, Q_BLOCK, axis=1).astype(jnp.float32)
        qpos = qs + jnp.arange(Q_BLOCK)
        z = jnp.einsum("bqhd,bshd->bhqs", qq, kf) * scale
        causal = kpos[None, :] < qpos[:, None]
        l1m = jnp.where(causal, jax.nn.log_sigmoid(-z), 0.0)
        r = lax.cumsum(l1m, axis=3, reverse=True) - l1m
        a = jnp.where(causal, jnp.exp(jax.nn.log_sigmoid(z) + r), 0.0)
        return jnp.einsum("bhqs,bshd->bqhd", a, vf).astype(q.dtype)

    out = lax.map(one, jnp.arange(S // Q_BLOCK))
    return out.transpose(1, 0, 2, 3, 4).reshape(B, S, H, d)


def sb_decode(q, k_new, v_new, cache_k, cache_v, page_table):
    DB, T, H, d = q.shape
    qf = q.astype(jnp.float32) * (d ** -0.5)
    z = jnp.einsum("bthd,bshd->bhts", qf, k_new.astype(jnp.float32))
    tt = jnp.arange(T)
    causal = tt[None, :] < tt[:, None]
    l1m = jnp.where(causal, jax.nn.log_sigmoid(-z), 0.0)
    r = lax.cumsum(l1m, axis=3, reverse=True) - l1m
    a = jnp.where(causal, jnp.exp(jax.nn.log_sigmoid(z) + r), 0.0)
    out = jnp.einsum("bhts,bshd->bhtd", a, v_new.astype(jnp.float32))
    surv = l1m.sum(axis=-1)

    def step(carry, pidx):
        acc, lsurv = carry
        kp = cache_k[pidx].astype(jnp.float32)
        vp = cache_v[pidx].astype(jnp.float32)
        zp = jnp.einsum("bthd,bshd->bhts", qf, kp)
        lp = jax.nn.log_sigmoid(-zp)
        rp = lax.cumsum(lp, axis=3, reverse=True) - lp + lsurv[..., None]
        ap = jnp.exp(jax.nn.log_sigmoid(zp) + rp)
        acc = acc + jnp.einsum("bhts,bshd->bhtd", ap, vp)
        return (acc, lsurv + lp.sum(axis=-1)), None

    (out, _), _ = lax.scan(step, (out, surv), page_table.T[::-1])
    return out.transpose(0, 2, 1, 3).astype(q.dtype)


def mem_kv(mem, g, w):
    B, M, _ = mem.shape
    k, v = jnp.split(rmsnorm(mem, g) @ w, 2, axis=-1)
    return k.reshape(B, M, N_HEADS_MEM, HEAD_DIM_MEM), v.reshape(B, M, N_HEADS_MEM, HEAD_DIM_MEM)


def mem_attend(q, k, v):
    s = jnp.einsum("bqhd,bmhd->bhqm", q.astype(jnp.float32), k.astype(jnp.float32)) * (HEAD_DIM_MEM ** -0.5)
    w = jax.nn.softmax(s, axis=-1)
    return jnp.einsum("bhqm,bmhd->bqhd", w, v.astype(jnp.float32)).astype(q.dtype)


def trunk_layer(x, pos, mixers, lw):
    n1, w1gu, w1d, nmix, w_in, w_br_a, w_br_b, w_br_m, w_out, n2, w2gu, w2d = lw
    B, S, _ = x.shape
    x = x + 0.5 * swiglu(rmsnorm(x, n1), w1gu, w1d)
    h = rmsnorm(x, nmix)
    q_a, k_a, v_a, q_b, k_b, v_b, q_m, gates = split_proj(h @ w_in)
    q_a = rope_partial(q_a, pos)
    k_a = rope_partial(k_a, pos)
    o_a, o_b, o_m = mixers(q_a, k_a, v_a, q_b, k_b, v_b, q_m)
    g = jax.nn.sigmoid(gates.astype(jnp.float32)).astype(x.dtype)
    merged = (g[:, :, 0] * (o_a.reshape(B, S, W_MOBA) @ w_br_a)
              + g[:, :, 1] * (o_b.reshape(B, S, W_SB) @ w_br_b)
              + g[:, :, 2] * (o_m.reshape(B, S, W_MEM) @ w_br_m))
    x = x + merged @ w_out
    x = x + 0.5 * swiglu(rmsnorm(x, n2), w2gu, w2d)
    return x, (k_a, v_a, k_b, v_b)


def setup_inputs(seed: int = 0) -> dict:
    key = jax.random.key(seed)
    ks = jax.random.split(key, 32)
    n_pages = PAST_LEN // PAGE_SIZE
    n_pool = (DEC_BATCH * n_pages * 5) // 4

    def nrm(k, shape, scale=1.0):
        return jax.random.normal(k, shape, jnp.float32) * scale

    def gain(k, shape):
        return 1.0 + 0.01 * jax.random.normal(k, shape, jnp.float32)

    moba_shape = (DEPTH, n_pool, PAGE_SIZE, N_HEADS_MOBA, HEAD_DIM)
    sb_shape = (DEPTH, n_pool, PAGE_SIZE, N_HEADS_SB, HEAD_DIM)
    mem_shape = (DEPTH, DEC_BATCH, N_MEM, N_HEADS_MEM, HEAD_DIM_MEM)
    page_table = jax.random.permutation(ks[6], n_pool)[: DEC_BATCH * n_pages].reshape(DEC_BATCH, n_pages).astype(jnp.int32)
    return {
        "x_prompt": nrm(ks[0], (BATCH, SEQ, D_MODEL)),
        "x_sample": nrm(ks[1], (DEC_BATCH, DEC_SEQ, D_MODEL)),
        "cache_k_moba": nrm(ks[2], moba_shape),
        "cache_v_moba": nrm(ks[3], moba_shape),
        "cache_k_sb": nrm(ks[4], sb_shape),
        "cache_v_sb": nrm(ks[5], sb_shape),
        "cache_mem_k": nrm(ks[7], mem_shape),
        "cache_mem_v": nrm(ks[8], mem_shape),
        "page_table": page_table,
        "mem_prompt": nrm(ks[9], (BATCH, N_MEM, D_MODEL)),
        "norm_ffn1": gain(ks[10], (DEPTH, D_MODEL)),
        "w_ffn1_gu": nrm(ks[11], (DEPTH, D_MODEL, 2 * D_FF), D_MODEL ** -0.5),
        "w_ffn1_down": nrm(ks[12], (DEPTH, D_FF, D_MODEL), D_FF ** -0.5),
        "norm_mix": gain(ks[13], (DEPTH, D_MODEL)),
        "w_in": nrm(ks[14], (DEPTH, D_MODEL, D_IN), D_MODEL ** -0.5),
        "norm_mem": gain(ks[15], (DEPTH, D_MODEL)),
        "w_mem_kv": nrm(ks[16], (DEPTH, D_MODEL, 2 * W_MEM), D_MODEL ** -0.5),
        "w_br_moba": nrm(ks[17], (DEPTH, W_MOBA, D_MODEL), W_MOBA ** -0.5),
        "w_br_sb": nrm(ks[18], (DEPTH, W_SB, D_MODEL), W_SB ** -0.5),
        "w_br_mem": nrm(ks[19], (DEPTH, W_MEM, D_MODEL), W_MEM ** -0.5),
        "w_out": nrm(ks[20], (DEPTH, D_MODEL, D_MODEL), D_MODEL ** -0.5),
        "norm_ffn2": gain(ks[21], (DEPTH, D_MODEL)),
        "w_ffn2_gu": nrm(ks[22], (DEPTH, D_MODEL, 2 * D_FF), D_MODEL ** -0.5),
        "w_ffn2_down": nrm(ks[23], (DEPTH, D_FF, D_MODEL), D_FF ** -0.5),
        "norm_final": gain(ks[24], (D_MODEL,)),
    }


def reference(x_prompt, x_sample, cache_k_moba, cache_v_moba, cache_k_sb, cache_v_sb, cache_mem_k, cache_mem_v,
              page_table, mem_prompt, norm_ffn1, w_ffn1_gu, w_ffn1_down, norm_mix, w_in, norm_mem, w_mem_kv,
              w_br_moba, w_br_sb, w_br_mem, w_out, norm_ffn2, w_ffn2_gu, w_ffn2_down, norm_final):
    S = x_prompt.shape[1]
    T = x_sample.shape[1]
    past = page_table.shape[1] * PAGE_SIZE
    pos_p = jnp.arange(S)
    pos_s = past + jnp.arange(T)
    xp, xs = x_prompt, x_sample
    kmp, vmp, ksp, vsp, mkp, mvp, kms, vms, kss, vss = [], [], [], [], [], [], [], [], [], []
    for l in range(DEPTH):
        lw = (norm_ffn1[l], w_ffn1_gu[l], w_ffn1_down[l], norm_mix[l], w_in[l], w_br_moba[l], w_br_sb[l],
              w_br_mem[l], w_out[l], norm_ffn2[l], w_ffn2_gu[l], w_ffn2_down[l])
        mk, mv = mem_kv(mem_prompt, norm_mem[l], w_mem_kv[l])
        mix_p = lambda qa, ka, va, qb, kb, vb, qm, mk=mk, mv=mv: (
            moba_prompt(qa, ka, va), sb_prompt(qb, kb, vb), mem_attend(qm, mk, mv))
        xp, (ka_p, va_p, kb_p, vb_p) = trunk_layer(xp, pos_p, mix_p, lw)
        mix_s = lambda qa, ka, va, qb, kb, vb, qm, l=l: (
            moba_decode(qa, ka, va, cache_k_moba[l], cache_v_moba[l], page_table),
            sb_decode(qb, kb, vb, cache_k_sb[l], cache_v_sb[l], page_table),
            mem_attend(qm, cache_mem_k[l], cache_mem_v[l]))
        xs, (ka_s, va_s, kb_s, vb_s) = trunk_layer(xs, pos_s, mix_s, lw)
        kmp.append(ka_p); vmp.append(va_p); ksp.append(kb_p); vsp.append(vb_p)
        mkp.append(mk); mvp.append(mv)
        kms.append(ka_s); vms.append(va_s); kss.append(kb_s); vss.append(vb_s)
    y_prompt = rmsnorm(xp, norm_final)
    y_sample = rmsnorm(xs, norm_final)
    return (y_prompt, y_sample, jnp.stack(kmp), jnp.stack(vmp), jnp.stack(ksp), jnp.stack(vsp),
            jnp.stack(mkp), jnp.stack(mvp), jnp.stack(kms), jnp.stack(vms), jnp.stack(kss), jnp.stack(vss))
```

```python
import functools

import jax
import jax.numpy as jnp
from jax import lax
from jax.experimental import pallas as pl
from jax.experimental.pallas import tpu as pltpu

F32 = jnp.float32
BF16 = jnp.bfloat16

D_MODEL = 1024
HEAD_DIM = 64
N_HEADS = 8
W_ATT = N_HEADS * HEAD_DIM
N_HEADS_MEM = 4
HEAD_DIM_MEM = 128
W_MEM = N_HEADS_MEM * HEAD_DIM_MEM
N_BRANCH = 3
D_IN = 6 * W_ATT + W_MEM + N_BRANCH * D_MODEL
D_FF = 2816
MOBA_BLOCK = 256
MOBA_TOPK = 3
PAGE_SIZE = 128
ROT_DIM = HEAD_DIM // 4
ROPE_THETA = 500000.0
EPS = 1e-6

NEG = -1e30
LOG_F32_TINY = -88.0
FF_CHUNK = 256
TOKEN_TILE = 256
LANES = 128
VMEM_LIMIT_BYTES = 56 * 1024 * 1024

_NT = (((1,), (1,)), ((), ()))


def _cparams(sem):
    return pltpu.CompilerParams(dimension_semantics=sem, vmem_limit_bytes=VMEM_LIMIT_BYTES)


def _rms(x, g):
    return x * lax.rsqrt(jnp.mean(x * x, axis=-1, keepdims=True) + EPS) * g


def _softplus(z):
    return jnp.maximum(z, 0.0) + jnp.log1p(jnp.exp(-jnp.abs(z)))


def _ffn_body(x_ref, g_ref, wg_ref, wu_ref, wd_ref, gf_ref, o_ref, h_ref, acc_ref, *, final_norm):
    c = pl.program_id(1)

    @pl.when(c == 0)
    def _():
        h_ref[...] = _rms(x_ref[...], g_ref[...]).astype(BF16)
        acc_ref[...] = jnp.zeros_like(acc_ref)

    h = h_ref[...]
    gate = jnp.dot(h, wg_ref[...], preferred_element_type=F32)
    up = jnp.dot(h, wu_ref[...], preferred_element_type=F32)
    act = (gate * jax.nn.sigmoid(gate) * up).astype(BF16)
    acc_ref[...] += jnp.dot(act, wd_ref[...], preferred_element_type=F32)

    @pl.when(c == pl.num_programs(1) - 1)
    def _():
        y = x_ref[...] + 0.5 * acc_ref[...]
        if final_norm:
            y = _rms(y, gf_ref[...])
        o_ref[...] = y


def _ffn(x, g, w_gu, w_down, g_final, *, final_norm, tm):
    n, d = x.shape
    nc = D_FF // FF_CHUNK
    return pl.pallas_call(
        functools.partial(_ffn_body, final_norm=final_norm),
        out_shape=jax.ShapeDtypeStruct((n, d), F32),
        grid=(n // tm, nc),
        in_specs=[
            pl.BlockSpec((tm, d), lambda i, c: (i, 0)),
            pl.BlockSpec((1, d), lambda i, c: (0, 0)),
            pl.BlockSpec((d, FF_CHUNK), lambda i, c: (0, c)),
            pl.BlockSpec((d, FF_CHUNK), lambda i, c: (0, nc + c)),
            pl.BlockSpec((FF_CHUNK, d), lambda i, c: (c, 0)),
            pl.BlockSpec((1, d), lambda i, c: (0, 0)),
        ],
        out_specs=pl.BlockSpec((tm, d), lambda i, c: (i, 0)),
        scratch_shapes=[pltpu.VMEM((tm, d), BF16), pltpu.VMEM((tm, d), F32)],
        compiler_params=_cparams(("parallel", "arbitrary")),
        name="ffn",
    )(x, g, w_gu, w_gu, w_down, g_final)


def _rope_tables(pos):
    half = ROT_DIM // 2
    inv = ROPE_THETA ** (-(jnp.arange(half, dtype=F32) * 2.0) / ROT_DIM)
    ang = pos.astype(F32)[:, None] * inv[None, :]
    cos, sin = jnp.cos(ang), jnp.sin(ang)
    t = pos.shape[0]
    rest = HEAD_DIM - ROT_DIM
    z8 = jnp.zeros((t, half), F32)
    cos_h = jnp.concatenate([cos, cos, jnp.ones((t, rest), F32)], axis=1)
    sa_h = jnp.concatenate([-sin, z8, jnp.zeros((t, rest), F32)], axis=1)
    sb_h = jnp.concatenate([z8, sin, jnp.zeros((t, rest), F32)], axis=1)
    two = lambda a: jnp.concatenate([a, a], axis=1)
    return two(cos_h), two(sa_h), two(sb_h)


def _rope(z, cos, sa, sb):
    half = ROT_DIM // 2
    w = z.shape[1]
    return z * cos + pltpu.roll(z, w - half, 1) * sa + pltpu.roll(z, half, 1) * sb


def _proj_common(x_ref, g_ref, w_ref, cos_ref, sa_ref, sb_ref):
    h = _rms(x_ref[...], g_ref[...]).astype(BF16)
    reps = W_ATT // LANES
    wide = lambda r: jnp.concatenate([r[...]] * reps, axis=1)
    cos, sa, sb = wide(cos_ref), wide(sa_ref), wide(sb_ref)

    def col(c):
        return jnp.dot(h, w_ref[:, c * W_ATT:(c + 1) * W_ATT], preferred_element_type=F32)

    rope = lambda z: _rope(z, cos, sa, sb)
    return col, rope


def _store_gates(col, gt_ref):
    for c in range(N_BRANCH * D_MODEL // W_ATT):
        gt_ref[:, c * W_ATT:(c + 1) * W_ATT] = jax.nn.sigmoid(col(7 + c)).astype(BF16)


def _proj_prompt_body(x_ref, g_ref, w_ref, cos_ref, sa_ref, sb_ref,
                      qaT_ref, kaT_ref, vaT_ref, kan_ref, vaTb_ref, kbar_ref,
                      qbT_ref, kbT_ref, vbT_ref, kbn_ref, vbTb_ref, qm_ref, gt_ref):
    col, rope = _proj_common(x_ref, g_ref, w_ref, cos_ref, sa_ref, sb_ref)
    qaT_ref[...] = rope(col(0)).T.astype(BF16)
    ka = rope(col(1))
    kaT_ref[...] = ka.T
    kan_ref[...] = ka.astype(BF16)
    kbar_ref[...] = jnp.sum(ka, axis=0, keepdims=True) * (1.0 / MOBA_BLOCK)
    vaT = col(2).T
    vaT_ref[...] = vaT
    vaTb_ref[...] = vaT.astype(BF16)
    qbT_ref[...] = col(3).T.astype(BF16)
    kb = col(4)
    kbT_ref[...] = kb.T
    kbn_ref[...] = kb.astype(BF16)
    vbT = col(5).T
    vbT_ref[...] = vbT
    vbTb_ref[...] = vbT.astype(BF16)
    qm_ref[...] = col(6).astype(BF16)
    _store_gates(col, gt_ref)


def _proj_sample_body(x_ref, g_ref, w_ref, cos_ref, sa_ref, sb_ref,
                      qa_ref, ka_ref, va_ref, qb_ref, kb_ref, vb_ref, qm_ref, gt_ref):
    col, rope = _proj_common(x_ref, g_ref, w_ref, cos_ref, sa_ref, sb_ref)
    qa_ref[...] = rope(col(0))
    ka_ref[...] = rope(col(1))
    va_ref[...] = col(2)
    qb_ref[...] = col(3)
    kb_ref[...] = col(4)
    vb_ref[...] = col(5)
    qm_ref[...] = col(6).astype(BF16)
    _store_gates(col, gt_ref)


def _proj_in_specs(tm, n_tab_tiles):
    tab = pl.BlockSpec((tm, LANES), lambda i: (i % n_tab_tiles, 0))
    return [
        pl.BlockSpec((tm, D_MODEL), lambda i: (i, 0)),
        pl.BlockSpec((1, D_MODEL), lambda i: (0, 0)),
        pl.BlockSpec((D_MODEL, D_IN), lambda i: (0, 0), pipeline_mode=pl.Buffered(1)),
        tab, tab, tab,
    ]


def _proj_prompt(x, g, w_in, tables, batch, seq):
    tm = TOKEN_TILE
    nb = seq // tm
    n = batch * seq
    sds = jax.ShapeDtypeStruct
    t_spec = pl.BlockSpec((None, W_ATT, tm), lambda i: (i // nb, 0, i % nb))
    blk_spec = pl.BlockSpec((None, None, W_ATT, tm), lambda i: (i // nb, i % nb, 0, 0))
    nat_spec = pl.BlockSpec((tm, W_ATT), lambda i: (i, 0))
    t_bf = sds((batch, W_ATT, seq), BF16)
    t_f32 = sds((batch, W_ATT, seq), F32)
    blk_bf = sds((batch, nb, W_ATT, tm), BF16)
    nat_bf = sds((n, W_ATT), BF16)
    out_shape = [t_bf, t_f32, t_f32, nat_bf, blk_bf, sds((n // tm, 1, W_ATT), F32),
                 t_bf, t_f32, t_f32, nat_bf, blk_bf, nat_bf, sds((n, N_BRANCH * D_MODEL), BF16)]
    out_specs = [t_spec, t_spec, t_spec, nat_spec, blk_spec,
                 pl.BlockSpec((None, 1, W_ATT), lambda i: (i, 0, 0)),
                 t_spec, t_spec, t_spec, nat_spec, blk_spec, nat_spec,
                 pl.BlockSpec((tm, N_BRANCH * D_MODEL), lambda i: (i, 0))]
    return pl.pallas_call(
        _proj_prompt_body,
        out_shape=out_shape,
        grid=(n // tm,),
        in_specs=_proj_in_specs(tm, nb),
        out_specs=out_specs,
        compiler_params=_cparams(("parallel",)),
        name="proj_prompt",
    )(x, g, w_in, *tables)


def _proj_sample(x, g, w_in, tables):
    n = x.shape[0]
    sds = jax.ShapeDtypeStruct
    nat = lambda dt: sds((n, W_ATT), dt)
    spec = pl.BlockSpec((n, W_ATT), lambda i: (0, 0))
    out_shape = [nat(F32)] * 6 + [nat(BF16), sds((n, N_BRANCH * D_MODEL), BF16)]
    out_specs = [spec] * 7 + [pl.BlockSpec((n, N_BRANCH * D_MODEL), lambda i: (0, 0))]
    return pl.pallas_call(
        _proj_sample_body,
        out_shape=out_shape,
        grid=(1,),
        in_specs=_proj_in_specs(n, 1),
        out_specs=out_specs,
        compiler_params=_cparams(("arbitrary",)),
        name="proj_sample",
    )(x, g, w_in, *tables)


def _normproj_body(x_ref, g_ref, w_ref, o_ref):
    h = _rms(x_ref[...], g_ref[...]).astype(BF16)
    o_ref[...] = jnp.dot(h, w_ref[...], preferred_element_type=F32)


def _normproj(x, g, w, tm):
    n, d = x.shape
    dout = w.shape[1]
    return pl.pallas_call(
        _normproj_body,
        out_shape=jax.ShapeDtypeStruct((n, dout), F32),
        grid=(n // tm,),
        in_specs=[pl.BlockSpec((tm, d), lambda i: (i, 0)),
                  pl.BlockSpec((1, d), lambda i: (0, 0)),
                  pl.BlockSpec((d, dout), lambda i: (0, 0))],
        out_specs=pl.BlockSpec((tm, dout), lambda i: (i, 0)),
        compiler_params=_cparams(("parallel",)),
        name="mem_kv",
    )(x, g, w)


def _head_masked(qT):
    row = lax.broadcasted_iota(jnp.int32, qT.shape, 0)
    zero = jnp.zeros_like(qT)
    return [jnp.where(row < HEAD_DIM, qT, zero), jnp.where(row >= HEAD_DIM, qT, zero)]


def _moba_prompt_body(qT_ref, k_ref, vT_ref, kbar_ref, oT_ref, sel_ref, *, nb):
    qi = pl.program_id(2)
    tq = qT_ref.shape[1]
    scale = HEAD_DIM ** -0.5
    qTs = _head_masked(qT_ref[...])
    kbar = kbar_ref[...].astype(BF16)
    nidx = lax.broadcasted_iota(jnp.int32, (nb, tq), 0)
    for hh in range(2):
        sc = jnp.dot(kbar, qTs[hh], preferred_element_type=F32)
        sc = jnp.where(nidx < qi, sc, -jnp.inf)
        rank = jnp.zeros((nb, tq), F32)
        for m in range(nb):
            r = sc[m:m + 1, :]
            beats = (r > sc) | ((r == sc) & (m < nidx))
            rank = rank + jnp.where(beats, 1.0, 0.0)
        sel_ref[hh] = jnp.where((nidx < qi) & (rank < MOBA_TOPK), 0.0, NEG)

    def block(n):
        start = pl.multiple_of(n * MOBA_BLOCK, MOBA_BLOCK)
        return k_ref[pl.ds(start, MOBA_BLOCK), :], vT_ref[n]

    kd, vd = block(qi)
    kpos = lax.broadcasted_iota(jnp.int32, (MOBA_BLOCK, tq), 0)
    qpos = lax.broadcasted_iota(jnp.int32, (MOBA_BLOCK, tq), 1)
    causal = jnp.where(kpos <= qpos, 0.0, NEG)
    init = []
    for hh in range(2):
        s = jnp.dot(kd, qTs[hh], preferred_element_type=F32) * scale + causal
        m = jnp.max(s, axis=0, keepdims=True)
        p = jnp.exp(s - m)
        l = jnp.sum(p, axis=0, keepdims=True)
        acc = jnp.dot(vd[hh * HEAD_DIM:(hh + 1) * HEAD_DIM, :], p.astype(BF16), preferred_element_type=F32)
        init.append((m, l, acc))

    def body(n, carry):
        kn, vn = block(n)
        out = []
        for hh in range(2):
            m, l, acc = carry[hh]
            s = jnp.dot(kn, qTs[hh], preferred_element_type=F32) * scale + sel_ref[hh, pl.ds(n, 1), :]
            m_new = jnp.maximum(m, jnp.max(s, axis=0, keepdims=True))
            alpha = jnp.exp(m - m_new)
            p = jnp.exp(s - m_new)
            l = alpha * l + jnp.sum(p, axis=0, keepdims=True)
            acc = alpha * acc + jnp.dot(vn[hh * HEAD_DIM:(hh + 1) * HEAD_DIM, :], p.astype(BF16),
                                        preferred_element_type=F32)
            out.append((m_new, l, acc))
        return tuple(out)

    fin = lax.fori_loop(0, qi, body, tuple(init))
    oT_ref[...] = jnp.concatenate([fin[hh][2] / fin[hh][1] for hh in range(2)], axis=0).astype(BF16)


def _sb_prompt_body(qT_ref, k_ref, vT_ref, oT_ref):
    qi = pl.program_id(2)
    tq = qT_ref.shape[1]
    scale = HEAD_DIM ** -0.5
    qTs = _head_masked(qT_ref[...])
    kpos = lax.broadcasted_iota(jnp.int32, (MOBA_BLOCK, tq), 0)
    qpos = lax.broadcasted_iota(jnp.int32, (MOBA_BLOCK, tq), 1)
    later = jnp.where(kpos < qpos, 1.0, 0.0).astype(BF16)

    def block(n):
        start = pl.multiple_of(n * MOBA_BLOCK, MOBA_BLOCK)
        return k_ref[pl.ds(start, MOBA_BLOCK), :], vT_ref[n]

    def step(kn, vn, lsurv, acc, hh, visible):
        z = jnp.dot(kn, qTs[hh], preferred_element_type=F32) * scale
        sp = _softplus(z)
        l = -sp if visible is None else jnp.where(visible, -sp, 0.0)
        r = jnp.dot(later, l.astype(BF16), preferred_element_type=F32) + lsurv
        a = jnp.exp(z - sp + r)
        if visible is not None:
            a = jnp.where(visible, a, 0.0)
        acc = acc + jnp.dot(vn[hh * HEAD_DIM:(hh + 1) * HEAD_DIM, :], a.astype(BF16),
                            preferred_element_type=F32)
        return lsurv + jnp.sum(l, axis=0, keepdims=True), acc

    kd, vd = block(qi)
    strict = kpos < qpos
    zero_l = jnp.zeros((1, tq), F32)
    zero_acc = jnp.zeros((HEAD_DIM, tq), F32)
    st = [step(kd, vd, zero_l, zero_acc, hh, strict) for hh in range(2)]

    def alive_flag(l0, l1):
        return (jnp.max(jnp.maximum(l0, l1)) > LOG_F32_TINY).astype(jnp.int32)

    def cond(c):
        return jnp.logical_and(c[0] >= 0, c[5] > 0)

    def body(c):
        n, l0, a0, l1, a1, _ = c
        kn, vn = block(n)
        l0, a0 = step(kn, vn, l0, a0, 0, None)
        l1, a1 = step(kn, vn, l1, a1, 1, None)
        return n - 1, l0, a0, l1, a1, alive_flag(l0, l1)

    fin = lax.while_loop(cond, body, (qi - 1, st[0][0], st[0][1], st[1][0], st[1][1],
                                      alive_flag(st[0][0], st[1][0])))
    oT_ref[...] = jnp.concatenate([fin[2], fin[4]], axis=0).astype(BF16)


def _prompt_mixer_specs(seq, nb):
    tq = TOKEN_TILE
    q_spec = pl.BlockSpec((None, 2 * HEAD_DIM, tq), lambda b, hp, qi: (b, hp, qi))
    k_spec = pl.BlockSpec((None, seq, 2 * HEAD_DIM), lambda b, hp, qi: (b, 0, hp))
    v_spec = pl.BlockSpec((None, nb, 2 * HEAD_DIM, MOBA_BLOCK), lambda b, hp, qi: (b, 0, hp, 0))
    return q_spec, k_spec, v_spec


def _moba_prompt(qT, k_nat, vT_blk, kbar):
    batch, _, seq = qT.shape
    nb = seq // MOBA_BLOCK
    q_spec, k_spec, v_spec = _prompt_mixer_specs(seq, nb)
    return pl.pallas_call(
        functools.partial(_moba_prompt_body, nb=nb),
        out_shape=jax.ShapeDtypeStruct(qT.shape, BF16),
        grid=(batch, N_HEADS // 2, seq // TOKEN_TILE),
        in_specs=[q_spec, k_spec, v_spec,
                  pl.BlockSpec((None, nb, 2 * HEAD_DIM), lambda b, hp, qi: (b, 0, hp))],
        out_specs=q_spec,
        scratch_shapes=[pltpu.VMEM((2, nb, TOKEN_TILE), F32)],
        compiler_params=_cparams(("parallel", "parallel", "arbitrary")),
        name="moba_prompt",
    )(qT, k_nat, vT_blk, kbar)


def _sb_prompt(qT, k_nat, vT_blk):
    batch, _, seq = qT.shape
    nb = seq // MOBA_BLOCK
    q_spec, k_spec, v_spec = _prompt_mixer_specs(seq, nb)
    return pl.pallas_call(
        _sb_prompt_body,
        out_shape=jax.ShapeDtypeStruct(qT.shape, BF16),
        grid=(batch, N_HEADS // 2, seq // TOKEN_TILE),
        in_specs=[q_spec, k_spec, v_spec],
        out_specs=q_spec,
        compiler_params=_cparams(("parallel", "parallel", "arbitrary")),
        name="sb_prompt",
    )(qT, k_nat, vT_blk)


def _mem_attend_body(q_ref, k_ref, v_ref, o_ref):
    q = q_ref[...]
    k = k_ref[...].astype(BF16)
    v = v_ref[...].astype(BF16)
    scale = HEAD_DIM_MEM ** -0.5
    outs = []
    for h in range(N_HEADS_MEM):
        sl = slice(h * HEAD_DIM_MEM, (h + 1) * HEAD_DIM_MEM)
        s = lax.dot_general(q[:, sl], k[:, sl], _NT, preferred_element_type=F32) * scale
        p = jnp.exp(s - jnp.max(s, axis=1, keepdims=True))
        o = jnp.dot(p.astype(BF16), v[:, sl], preferred_element_type=F32)
        outs.append(o / jnp.sum(p, axis=1, keepdims=True))
    o_ref[...] = jnp.concatenate(outs, axis=1).astype(BF16)


def _mem_attend(q, k, v, tq):
    batch, seq, w = q.shape
    n_mem = k.shape[1]
    kv_spec = pl.BlockSpec((None, n_mem, w), lambda b, i: (b, 0, 0))
    q_spec = pl.BlockSpec((None, tq, w), lambda b, i: (b, i, 0))
    return pl.pallas_call(
        _mem_attend_body,
        out_shape=jax.ShapeDtypeStruct(q.shape, BF16),
        grid=(batch, seq // tq),
        in_specs=[q_spec, kv_spec, kv_spec],
        out_specs=q_spec,
        compiler_params=_cparams(("parallel", "parallel")),
        name="mem_attend",
    )(q, k, v)


def _merge_body(x_ref, oa_ref, ob_ref, om_ref, gt_ref, wa_ref, wb_ref, wm_ref, wo_ref, o_ref):
    merged = None
    for i, (o, w) in enumerate(((oa_ref, wa_ref), (ob_ref, wb_ref), (om_ref, wm_ref))):
        br = jnp.dot(o[...], w[...], preferred_element_type=F32)
        term = gt_ref[:, i * D_MODEL:(i + 1) * D_MODEL].astype(F32) * br
        merged = term if merged is None else merged + term
    o_ref[...] = x_ref[...] + jnp.dot(merged.astype(BF16), wo_ref[...], preferred_element_type=F32)


def _merge(x, oa, ob, om, gates, wa, wb, wm, wo, tm):
    n, d = x.shape
    row = lambda w: pl.BlockSpec((tm, w), lambda i: (i, 0))
    full = lambda a: pl.BlockSpec(a.shape, lambda i: (0, 0))
    return pl.pallas_call(
        _merge_body,
        out_shape=jax.ShapeDtypeStruct((n, d), F32),
        grid=(n // tm,),
        in_specs=[row(d), row(W_ATT), row(W_ATT), row(W_MEM), row(N_BRANCH * d),
                  full(wa), full(wb), full(wm), full(wo)],
        out_specs=row(d),
        compiler_params=_cparams(("parallel",)),
        name="merge",
    )(x, oa, ob, om, gates, wa, wb, wm, wo)


PAGES_PER_STEP = 4


def _own_head_columns(acc, n_tok):
    lane_head = lax.broadcasted_iota(jnp.int32, (n_tok, W_ATT), 1) // HEAD_DIM
    out = jnp.zeros((n_tok, W_ATT), F32)
    for h in range(N_HEADS):
        out = out + jnp.where(lane_head == h, acc[h * n_tok:(h + 1) * n_tok, :], 0.0)
    return out


def _suffix_sum_lanes(x):
    lane = lax.broadcasted_iota(jnp.int32, x.shape, 1)
    n = x.shape[1]
    sh = 1
    while sh < n:
        x = x + jnp.where(lane < n - sh, pltpu.roll(x, n - sh, 1), 0.0)
        sh *= 2
    return x


def _sb_decode_body(pt_ref, qbd_ref, knT_ref, vnT_ref, *refs, n_tok, n_pages):
    g = PAGES_PER_STEP
    k_refs, v_refs = refs[:g], refs[g:2 * g]
    o_ref, acc_ref, ls_ref = refs[2 * g:]
    step = pl.program_id(1)
    scale = HEAD_DIM ** -0.5
    qbd = qbd_ref[...]
    rows = N_HEADS * n_tok

    def page(kT, vT, lsurv, visible):
        z = jnp.dot(qbd, kT.astype(BF16), preferred_element_type=F32) * scale
        sp = _softplus(z)
        l = -sp if visible is None else jnp.where(visible, -sp, 0.0)
        suffix = _suffix_sum_lanes(l)
        a = jnp.exp(z - sp + (suffix - l) + lsurv)
        if visible is not None:
            a = jnp.where(visible, a, 0.0)
        o = lax.dot_general(a.astype(BF16), vT.astype(BF16), _NT, preferred_element_type=F32)
        return o, lsurv + suffix[:, 0:1]

    @pl.when(step == 0)
    def _():
        key = lax.broadcasted_iota(jnp.int32, (rows, PAGE_SIZE), 1)
        tok = lax.broadcasted_iota(jnp.int32, (rows, PAGE_SIZE), 0) % n_tok
        o, ls = page(knT_ref[...], vnT_ref[...], jnp.zeros((rows, 1), F32), key < tok)
        acc_ref[...] = o
        ls_ref[...] = ls

    @pl.when(jnp.max(ls_ref[...]) > LOG_F32_TINY)
    def _():
        acc = acc_ref[...]
        ls = ls_ref[...]
        for i in range(g):
            o, ls = page(k_refs[i][...], v_refs[i][...], ls, None)
            acc = acc + o
        acc_ref[...] = acc
        ls_ref[...] = ls

    @pl.when(step == n_pages // g - 1)
    def _():
        o_ref[...] = _own_head_columns(acc_ref[...], n_tok).astype(BF16)


def _sb_decode(page_table, qbd, knT, vnT, ckT, cvT, n_tok):
    batch, n_pages = page_table.shape
    g = PAGES_PER_STEP
    rows = N_HEADS * n_tok

    def page_spec(i):
        return pl.BlockSpec((None, W_ATT, PAGE_SIZE),
                            lambda b, s, pt: (pt[b, n_pages - 1 - (s * g + i)], 0, 0))

    per_b = lambda r, c: pl.BlockSpec((None, r, c), lambda b, s, pt: (b, 0, 0))
    return pl.pallas_call(
        functools.partial(_sb_decode_body, n_tok=n_tok, n_pages=n_pages),
        out_shape=jax.ShapeDtypeStruct((batch, n_tok, W_ATT), BF16),
        grid_spec=pltpu.PrefetchScalarGridSpec(
            num_scalar_prefetch=1,
            grid=(batch, n_pages // g),
            in_specs=[per_b(rows, W_ATT), per_b(W_ATT, PAGE_SIZE), per_b(W_ATT, PAGE_SIZE)]
            + [page_spec(i) for i in range(g)] * 2,
            out_specs=per_b(n_tok, W_ATT),
            scratch_shapes=[pltpu.VMEM((rows, W_ATT), F32), pltpu.VMEM((rows, 1), F32)],
        ),
        compiler_params=_cparams(("parallel", "arbitrary")),
        name="sb_decode",
    )(page_table, qbd, knT, vnT, *([ckT] * g), *([cvT] * g))


def _moba_select_body(pt_ref, qbd_ref, *refs, n_pages):
    g = PAGES_PER_STEP
    k_refs = refs[:g]
    idx_ref, sc_ref = refs[g:]
    step = pl.program_id(1)
    ppb = MOBA_BLOCK // PAGE_SIZE
    qbd = qbd_ref[...]
    rows = qbd.shape[0]
    lane = lax.broadcasted_iota(jnp.int32, (rows, LANES), 1)

    @pl.when(step == 0)
    def _():
        sc_ref[...] = jnp.zeros_like(sc_ref)

    sc = sc_ref[...]
    for i in range(g):
        z = jnp.dot(qbd, k_refs[i][...].astype(BF16), preferred_element_type=F32)
        blk = (step * g + i) // ppb
        sc = sc + jnp.where(lane == blk, jnp.sum(z, axis=1, keepdims=True), 0.0)
    sc_ref[...] = sc

    @pl.when(step == n_pages // g - 1)
    def _():
        nb = n_pages // ppb
        s = jnp.where(lane < nb, sc_ref[...], -jnp.inf)
        out = jnp.zeros((rows, LANES), jnp.int32)
        for j in range(MOBA_TOPK):
            best = jnp.max(s, axis=1, keepdims=True)
            pick = jnp.min(jnp.where(s == best, lane, LANES), axis=1, keepdims=True)
            out = jnp.where(lane == j, pick, out)
            s = jnp.where(lane == pick, -jnp.inf, s)
        idx_ref[...] = out


def _moba_select(page_table, qbd, ckT):
    batch, n_pages = page_table.shape
    g = PAGES_PER_STEP
    rows = qbd.shape[1]

    def page_spec(i):
        return pl.BlockSpec((None, W_ATT, PAGE_SIZE), lambda b, s, pt: (pt[b, s * g + i], 0, 0))

    return pl.pallas_call(
        functools.partial(_moba_select_body, n_pages=n_pages),
        out_shape=jax.ShapeDtypeStruct((batch, rows, LANES), jnp.int32),
        grid_spec=pltpu.PrefetchScalarGridSpec(
            num_scalar_prefetch=1,
            grid=(batch, n_pages // g),
            in_specs=[pl.BlockSpec((None, rows, W_ATT), lambda b, s, pt: (b, 0, 0))]
            + [page_spec(i) for i in range(g)],
            out_specs=pl.BlockSpec((None, rows, LANES), lambda b, s, pt: (b, 0, 0)),
            scratch_shapes=[pltpu.VMEM((rows, LANES), F32)],
        ),
        compiler_params=_cparams(("parallel", "arbitrary")),
        name="moba_select",
    )(page_table, qbd, *([ckT] * g))


def _moba_gather_body(pt_ref, idx_ref, q_ref, knT_ref, vnT_ref, ck_hbm, cv_hbm, o_ref,
                      kbuf, vbuf, sem, *, n_tok, n_pages):
    b = pl.program_id(0)
    h = pl.program_id(1)
    ppb = MOBA_BLOCK // PAGE_SIZE
    n_slots = n_tok * MOBA_TOPK
    scale = HEAD_DIM ** -0.5

    def copies(slot):
        tok, j = divmod(slot, MOBA_TOPK)
        blk = idx_ref[((b * N_HEADS + h) * n_tok + tok) * MOBA_TOPK + j]
        out = []
        for i in range(ppb):
            pg = pt_ref[b * n_pages + blk * ppb + i]
            dst = pl.ds(i * PAGE_SIZE, PAGE_SIZE)
            out.append(pltpu.make_async_copy(ck_hbm.at[pg, h], kbuf.at[slot, :, dst], sem.at[0]))
            out.append(pltpu.make_async_copy(cv_hbm.at[pg, h], vbuf.at[slot, :, dst], sem.at[1]))
        return out

    all_copies = [c for slot in range(n_slots) for c in copies(slot)]
    for c in all_copies:
        c.start()
    for c in all_copies:
        c.wait()

    q = q_ref[...]
    tok_row = lax.broadcasted_iota(jnp.int32, (n_tok, MOBA_BLOCK), 0)
    key = lax.broadcasted_iota(jnp.int32, (n_tok, PAGE_SIZE), 1)
    tok = lax.broadcasted_iota(jnp.int32, (n_tok, PAGE_SIZE), 0)
    s_own = jnp.dot(q, knT_ref[...], preferred_element_type=F32) * scale + jnp.where(key <= tok, 0.0, NEG)
    logits = []
    for slot in range(n_slots):
        s = jnp.dot(q, kbuf[slot].astype(BF16), preferred_element_type=F32) * scale
        logits.append(s + jnp.where(tok_row == slot // MOBA_TOPK, 0.0, NEG))
    m = jnp.max(s_own, axis=1, keepdims=True)
    for s in logits:
        m = jnp.maximum(m, jnp.max(s, axis=1, keepdims=True))
    p_own = jnp.exp(s_own - m)
    den = jnp.sum(p_own, axis=1, keepdims=True)
    acc = lax.dot_general(p_own.astype(BF16), vnT_ref[...], _NT, preferred_element_type=F32)
    for slot, s in enumerate(logits):
        p = jnp.exp(s - m)
        den = den + jnp.sum(p, axis=1, keepdims=True)
        acc = acc + lax.dot_general(p.astype(BF16), vbuf[slot].astype(BF16), _NT, preferred_element_type=F32)
    o_ref[...] = (acc / den).astype(BF16)


def _moba_gather(pt_flat, idx_flat, q, knT, vnT, ckT, cvT, n_pages):
    batch, _, n_tok, _ = q.shape
    n_slots = n_tok * MOBA_TOPK
    per_bh = lambda r, c: pl.BlockSpec((None, None, r, c), lambda b, h, pt, ix: (b, h, 0, 0))
    return pl.pallas_call(
        functools.partial(_moba_gather_body, n_tok=n_tok, n_pages=n_pages),
        out_shape=jax.ShapeDtypeStruct((batch, N_HEADS, n_tok, HEAD_DIM), BF16),
        grid_spec=pltpu.PrefetchScalarGridSpec(
            num_scalar_prefetch=2,
            grid=(batch, N_HEADS),
            in_specs=[per_bh(n_tok, HEAD_DIM), per_bh(HEAD_DIM, PAGE_SIZE), per_bh(HEAD_DIM, PAGE_SIZE),
                      pl.BlockSpec(memory_space=pl.ANY), pl.BlockSpec(memory_space=pl.ANY)],
            out_specs=per_bh(n_tok, HEAD_DIM),
            scratch_shapes=[pltpu.VMEM((n_slots, HEAD_DIM, MOBA_BLOCK), F32),
                            pltpu.VMEM((n_slots, HEAD_DIM, MOBA_BLOCK), F32),
                            pltpu.SemaphoreType.DMA((2,))],
        ),
        compiler_params=_cparams(("arbitrary", "arbitrary")),
        name="moba_gather",
    )(pt_flat, idx_flat, q, knT, vnT, ckT, cvT)


def _block_diag_q(q, n_tok):
    batch = q.shape[0] // n_tok
    q3 = q.reshape(batch, 1, n_tok, W_ATT)
    col_head = (jnp.arange(W_ATT) // HEAD_DIM)[None, None, None, :]
    row_head = jnp.arange(N_HEADS)[None, :, None, None]
    return jnp.where(col_head == row_head, q3, 0.0).reshape(batch, N_HEADS * n_tok, W_ATT)


def _new_tokens_T(x, n_tok):
    batch = x.shape[0] // n_tok
    xt = x.reshape(batch, n_tok, W_ATT).transpose(0, 2, 1)
    return jnp.pad(xt, ((0, 0), (0, 0), (0, PAGE_SIZE - n_tok)))


def _cache_T(c):
    return c.transpose(0, 2, 3, 1)


def kernel(x_prompt, x_sample, cache_k_moba, cache_v_moba, cache_k_sb, cache_v_sb, cache_mem_k, cache_mem_v,
           page_table, mem_prompt, norm_ffn1, w_ffn1_gu, w_ffn1_down, norm_mix, w_in, norm_mem, w_mem_kv,
           w_br_moba, w_br_sb, w_br_mem, w_out, norm_ffn2, w_ffn2_gu, w_ffn2_down, norm_final):
    batch, seq, d = x_prompt.shape
    dec_batch, n_tok, _ = x_sample.shape
    n_pages = page_table.shape[1]
    past = n_pages * PAGE_SIZE
    depth = w_in.shape[0]
    assert depth == 1 and past % MOBA_BLOCK == 0 and seq % TOKEN_TILE == 0

    lyr = 0
    bf = lambda w: w[lyr].astype(BF16)
    row = lambda g: g.reshape(1, d)
    w1gu, w1d, w2gu, w2d = bf(w_ffn1_gu), bf(w_ffn1_down), bf(w_ffn2_gu), bf(w_ffn2_down)
    win, wmem, wa, wb, wm, wo = bf(w_in), bf(w_mem_kv), bf(w_br_moba), bf(w_br_sb), bf(w_br_mem), bf(w_out)
    n1, nmix, nmem, n2, nfin = row(norm_ffn1[lyr]), row(norm_mix[lyr]), row(norm_mem[lyr]), row(norm_ffn2[lyr]), row(norm_final)

    n = batch * seq
    nb = seq // MOBA_BLOCK
    xp = _ffn(x_prompt.reshape(n, d), n1, w1gu, w1d, nfin, final_norm=False, tm=1024)
    (qaT, kaT, vaT, ka_nat, vaT_blk, kbar, qbT, kbT, vbT, kb_nat, vbT_blk, qm, gates) = _proj_prompt(
        xp, nmix, win, _rope_tables(jnp.arange(seq)), batch, seq)
    mem_kv = _normproj(mem_prompt.reshape(-1, d), nmem, wmem, tm=512)
    n_mem = mem_prompt.shape[1]
    mk = mem_kv[:, :W_MEM].reshape(batch, n_mem, W_MEM)
    mv = mem_kv[:, W_MEM:].reshape(batch, n_mem, W_MEM)
    oaT = _moba_prompt(qaT, ka_nat.reshape(batch, seq, W_ATT), vaT_blk, kbar.reshape(batch, nb, W_ATT))
    obT = _sb_prompt(qbT, kb_nat.reshape(batch, seq, W_ATT), vbT_blk)
    om = _mem_attend(qm.reshape(batch, seq, W_MEM), mk, mv, tq=512)
    nat = lambda t: t.transpose(0, 2, 1).reshape(n, W_ATT)
    xp = _merge(xp, nat(oaT), nat(obT), om.reshape(n, W_MEM), gates, wa, wb, wm, wo, tm=512)
    y_prompt = _ffn(xp, n2, w2gu, w2d, nfin, final_norm=True, tm=1024).reshape(batch, seq, d)
    rows_out = lambda t: t.reshape(batch, N_HEADS, HEAD_DIM, seq).transpose(0, 3, 1, 2)[None]
    mem_out = lambda t: t.reshape(1, batch, n_mem, N_HEADS_MEM, HEAD_DIM_MEM)

    ns = dec_batch * n_tok
    xs = _ffn(x_sample.reshape(ns, d), n1, w1gu, w1d, nfin, final_norm=False, tm=ns)
    pos_s = jnp.tile(past + jnp.arange(n_tok), dec_batch)
    qa, ka, va, qb, kb, vb, qm_s, gates_s = _proj_sample(xs, nmix, win, _rope_tables(pos_s))

    ck_a, cv_a = _cache_T(cache_k_moba[lyr]), _cache_T(cache_v_moba[lyr])
    ck_b, cv_b = _cache_T(cache_k_sb[lyr]), _cache_T(cache_v_sb[lyr])
    pool = ck_a.shape[0]
    flat = lambda c: c.reshape(pool, W_ATT, PAGE_SIZE)

    idx = _moba_select(page_table, _block_diag_q(qa, n_tok).astype(BF16), flat(ck_a))
    heads = lambda t: t.reshape(dec_batch, n_tok, N_HEADS, HEAD_DIM).transpose(0, 2, 1, 3)
    heads_T = lambda t: _new_tokens_T(t, n_tok).reshape(dec_batch, N_HEADS, HEAD_DIM, PAGE_SIZE)
    oa_s = _moba_gather(page_table.reshape(-1), idx[:, :, :MOBA_TOPK].reshape(-1),
                        heads(qa).astype(BF16), heads_T(ka).astype(BF16), heads_T(va).astype(BF16),
                        ck_a, cv_a, n_pages)
    oa_s = oa_s.transpose(0, 2, 1, 3).reshape(ns, W_ATT)
    ob_s = _sb_decode(page_table, _block_diag_q(qb, n_tok).astype(BF16),
                      _new_tokens_T(kb, n_tok).astype(BF16), _new_tokens_T(vb, n_tok).astype(BF16),
                      flat(ck_b), flat(cv_b), n_tok).reshape(ns, W_ATT)
    om_s = _mem_attend(qm_s.reshape(dec_batch, n_tok, W_MEM),
                       cache_mem_k[lyr].reshape(dec_batch, -1, W_MEM),
                       cache_mem_v[lyr].reshape(dec_batch, -1, W_MEM), tq=n_tok).reshape(ns, W_MEM)
    xs = _merge(xs, oa_s, ob_s, om_s, gates_s, wa, wb, wm, wo, tm=ns)
    y_sample = _ffn(xs, n2, w2gu, w2d, nfin, final_norm=True, tm=ns).reshape(dec_batch, n_tok, d)
    new_rows = lambda t: t.reshape(1, dec_batch, n_tok, N_HEADS, HEAD_DIM)

    return (y_prompt, y_sample, rows_out(kaT), rows_out(vaT), rows_out(kbT), rows_out(vbT),
            mem_out(mk), mem_out(mv), new_rows(ka), new_rows(va), new_rows(kb), new_rows(vb))
```

```python
import functools

import jax
import jax.numpy as jnp
from jax import lax
from jax.experimental import pallas as pl
from jax.experimental.pallas import tpu as pltpu

F32 = jnp.float32
BF16 = jnp.bfloat16

D_MODEL = 1024
HEAD_DIM = 64
N_HEADS = 8
W_ATT = N_HEADS * HEAD_DIM
N_HEADS_MEM = 4
HEAD_DIM_MEM = 128
W_MEM = N_HEADS_MEM * HEAD_DIM_MEM
N_BRANCH = 3
D_IN = 6 * W_ATT + W_MEM + N_BRANCH * D_MODEL
D_FF = 2816
MOBA_BLOCK = 256
MOBA_TOPK = 3
PAGE_SIZE = 128
ROT_DIM = HEAD_DIM // 4
ROPE_THETA = 500000.0
EPS = 1e-6

NEG = -1e30
LOG_F32_TINY = -88.0
FF_CHUNK = 256
TOKEN_TILE = 256
LANES = 128
VMEM_LIMIT_BYTES = 56 * 1024 * 1024

_NT = (((1,), (1,)), ((), ()))


def _cparams(sem):
    return pltpu.CompilerParams(dimension_semantics=sem, vmem_limit_bytes=VMEM_LIMIT_BYTES)


def _rms(x, g):
    return x * lax.rsqrt(jnp.mean(x * x, axis=-1, keepdims=True) + EPS) * g


def _log_sigmoid_pair(z):
    t = jnp.log(1.0 + jnp.exp(-jnp.abs(z)))
    return -jnp.maximum(z, 0.0) - t, jnp.minimum(z, 0.0) - t


def _ffn_body(x_ref, g_ref, wg_ref, wu_ref, wd_ref, gf_ref, o_ref, h_ref, acc_ref, *, final_norm):
    c = pl.program_id(1)

    @pl.when(c == 0)
    def _():
        h_ref[...] = _rms(x_ref[...], g_ref[...]).astype(BF16)
        acc_ref[...] = jnp.zeros_like(acc_ref)

    h = h_ref[...]
    gate = jnp.dot(h, wg_ref[...], preferred_element_type=F32)
    up = jnp.dot(h, wu_ref[...], preferred_element_type=F32)
    act = (gate * jax.nn.sigmoid(gate) * up).astype(BF16)
    acc_ref[...] += jnp.dot(act, wd_ref[...], preferred_element_type=F32)

    @pl.when(c == pl.num_programs(1) - 1)
    def _():
        y = x_ref[...] + 0.5 * acc_ref[...]
        if final_norm:
            y = _rms(y, gf_ref[...])
        o_ref[...] = y


def _ffn(x, g, w_gu, w_down, g_final, *, final_norm, tm):
    n, d = x.shape
    nc = D_FF // FF_CHUNK
    return pl.pallas_call(
        functools.partial(_ffn_body, final_norm=final_norm),
        out_shape=jax.ShapeDtypeStruct((n, d), F32),
        grid=(n // tm, nc),
        in_specs=[
            pl.BlockSpec((tm, d), lambda i, c: (i, 0)),
            pl.BlockSpec((1, d), lambda i, c: (0, 0)),
            pl.BlockSpec((d, FF_CHUNK), lambda i, c: (0, c)),
            pl.BlockSpec((d, FF_CHUNK), lambda i, c: (0, nc + c)),
            pl.BlockSpec((FF_CHUNK, d), lambda i, c: (c, 0)),
            pl.BlockSpec((1, d), lambda i, c: (0, 0)),
        ],
        out_specs=pl.BlockSpec((tm, d), lambda i, c: (i, 0)),
        scratch_shapes=[pltpu.VMEM((tm, d), BF16), pltpu.VMEM((tm, d), F32)],
        compiler_params=_cparams(("parallel", "arbitrary")),
        name="ffn",
    )(x, g, w_gu, w_gu, w_down, g_final)


def _rope_tables(pos):
    half = ROT_DIM // 2
    inv = ROPE_THETA ** (-(jnp.arange(half, dtype=F32) * 2.0) / ROT_DIM)
    ang = pos.astype(F32)[:, None] * inv[None, :]
    cos, sin = jnp.cos(ang), jnp.sin(ang)
    t = pos.shape[0]
    rest = HEAD_DIM - ROT_DIM
    z8 = jnp.zeros((t, half), F32)
    cos_h = jnp.concatenate([cos, cos, jnp.ones((t, rest), F32)], axis=1)
    sa_h = jnp.concatenate([-sin, z8, jnp.zeros((t, rest), F32)], axis=1)
    sb_h = jnp.concatenate([z8, sin, jnp.zeros((t, rest), F32)], axis=1)
    two = lambda a: jnp.concatenate([a, a], axis=1)
    return two(cos_h), two(sa_h), two(sb_h)


def _rope(z, cos, sa, sb):
    half = ROT_DIM // 2
    w = z.shape[1]
    return z * cos + pltpu.roll(z, w - half, 1) * sa + pltpu.roll(z, half, 1) * sb


def _proj_common(x_ref, g_ref, w_ref, cos_ref, sa_ref, sb_ref):
    h = _rms(x_ref[...], g_ref[...]).astype(BF16)
    reps = W_ATT // LANES
    wide = lambda r: jnp.concatenate([r[...]] * reps, axis=1)
    cos, sa, sb = wide(cos_ref), wide(sa_ref), wide(sb_ref)

    def col(c):
        return jnp.dot(h, w_ref[:, c * W_ATT:(c + 1) * W_ATT], preferred_element_type=F32)

    rope = lambda z: _rope(z, cos, sa, sb)
    return col, rope


def _store_gates(col, gt_ref):
    for c in range(N_BRANCH * D_MODEL // W_ATT):
        gt_ref[:, c * W_ATT:(c + 1) * W_ATT] = jax.nn.sigmoid(col(7 + c)).astype(BF16)


def _proj_prompt_body(x_ref, g_ref, w_ref, cos_ref, sa_ref, sb_ref,
                      qaT_ref, kaT_ref, vaT_ref, kan_ref, vaTb_ref, kbar_ref,
                      qbT_ref, kbT_ref, vbT_ref, kbn_ref, vbTb_ref, qm_ref, gt_ref):
    col, rope = _proj_common(x_ref, g_ref, w_ref, cos_ref, sa_ref, sb_ref)
    qaT_ref[...] = rope(col(0)).T.astype(BF16)
    ka = rope(col(1))
    kaT_ref[...] = ka.T
    kan_ref[...] = ka.astype(BF16)
    kbar_ref[...] = jnp.sum(ka, axis=0, keepdims=True) * (1.0 / MOBA_BLOCK)
    vaT = col(2).T
    vaT_ref[...] = vaT
    vaTb_ref[...] = vaT.astype(BF16)
    qbT_ref[...] = col(3).T.astype(BF16)
    kb = col(4)
    kbT_ref[...] = kb.T
    kbn_ref[...] = kb.astype(BF16)
    vbT = col(5).T
    vbT_ref[...] = vbT
    vbTb_ref[...] = vbT.astype(BF16)
    qm_ref[...] = col(6).astype(BF16)
    _store_gates(col, gt_ref)


def _proj_sample_body(x_ref, g_ref, w_ref, cos_ref, sa_ref, sb_ref,
                      qa_ref, ka_ref, va_ref, qb_ref, kb_ref, vb_ref, qm_ref, gt_ref):
    col, rope = _proj_common(x_ref, g_ref, w_ref, cos_ref, sa_ref, sb_ref)
    qa_ref[...] = rope(col(0))
    ka_ref[...] = rope(col(1))
    va_ref[...] = col(2)
    qb_ref[...] = col(3)
    kb_ref[...] = col(4)
    vb_ref[...] = col(5)
    qm_ref[...] = col(6).astype(BF16)
    _store_gates(col, gt_ref)


def _proj_in_specs(tm, n_tab_tiles):
    tab = pl.BlockSpec((tm, LANES), lambda i: (i % n_tab_tiles, 0))
    return [
        pl.BlockSpec((tm, D_MODEL), lambda i: (i, 0)),
        pl.BlockSpec((1, D_MODEL), lambda i: (0, 0)),
        pl.BlockSpec((D_MODEL, D_IN), lambda i: (0, 0), pipeline_mode=pl.Buffered(1)),
        tab, tab, tab,
    ]


def _proj_prompt(x, g, w_in, tables, batch, seq):
    tm = TOKEN_TILE
    nb = seq // tm
    n = batch * seq
    sds = jax.ShapeDtypeStruct
    t_spec = pl.BlockSpec((None, W_ATT, tm), lambda i: (i // nb, 0, i % nb))
    blk_spec = pl.BlockSpec((None, None, W_ATT, tm), lambda i: (i // nb, i % nb, 0, 0))
    nat_spec = pl.BlockSpec((tm, W_ATT), lambda i: (i, 0))
    t_bf = sds((batch, W_ATT, seq), BF16)
    t_f32 = sds((batch, W_ATT, seq), F32)
    blk_bf = sds((batch, nb, W_ATT, tm), BF16)
    nat_bf = sds((n, W_ATT), BF16)
    out_shape = [t_bf, t_f32, t_f32, nat_bf, blk_bf, sds((n // tm, 1, W_ATT), F32),
                 t_bf, t_f32, t_f32, nat_bf, blk_bf, nat_bf, sds((n, N_BRANCH * D_MODEL), BF16)]
    out_specs = [t_spec, t_spec, t_spec, nat_spec, blk_spec,
                 pl.BlockSpec((None, 1, W_ATT), lambda i: (i, 0, 0)),
                 t_spec, t_spec, t_spec, nat_spec, blk_spec, nat_spec,
                 pl.BlockSpec((tm, N_BRANCH * D_MODEL), lambda i: (i, 0))]
    return pl.pallas_call(
        _proj_prompt_body,
        out_shape=out_shape,
        grid=(n // tm,),
        in_specs=_proj_in_specs(tm, nb),
        out_specs=out_specs,
        compiler_params=_cparams(("parallel",)),
        name="proj_prompt",
    )(x, g, w_in, *tables)


def _proj_sample(x, g, w_in, tables):
    n = x.shape[0]
    sds = jax.ShapeDtypeStruct
    nat = lambda dt: sds((n, W_ATT), dt)
    spec = pl.BlockSpec((n, W_ATT), lambda i: (0, 0))
    out_shape = [nat(F32)] * 6 + [nat(BF16), sds((n, N_BRANCH * D_MODEL), BF16)]
    out_specs = [spec] * 7 + [pl.BlockSpec((n, N_BRANCH * D_MODEL), lambda i: (0, 0))]
    return pl.pallas_call(
        _proj_sample_body,
        out_shape=out_shape,
        grid=(1,),
        in_specs=_proj_in_specs(n, 1),
        out_specs=out_specs,
        compiler_params=_cparams(("arbitrary",)),
        name="proj_sample",
    )(x, g, w_in, *tables)


def _normproj_body(x_ref, g_ref, w_ref, o_ref):
    h = _rms(x_ref[...], g_ref[...]).astype(BF16)
    o_ref[...] = jnp.dot(h, w_ref[...], preferred_element_type=F32)


def _normproj(x, g, w, tm):
    n, d = x.shape
    dout = w.shape[1]
    return pl.pallas_call(
        _normproj_body,
        out_shape=jax.ShapeDtypeStruct((n, dout), F32),
        grid=(n // tm,),
        in_specs=[pl.BlockSpec((tm, d), lambda i: (i, 0)),
                  pl.BlockSpec((1, d), lambda i: (0, 0)),
                  pl.BlockSpec((d, dout), lambda i: (0, 0))],
        out_specs=pl.BlockSpec((tm, dout), lambda i: (i, 0)),
        compiler_params=_cparams(("parallel",)),
        name="mem_kv",
    )(x, g, w)


def _head_masked(qT):
    row = lax.broadcasted_iota(jnp.int32, qT.shape, 0)
    zero = jnp.zeros_like(qT)
    return [jnp.where(row < HEAD_DIM, qT, zero), jnp.where(row >= HEAD_DIM, qT, zero)]


def _moba_prompt_body(qT_ref, k_ref, vT_ref, kbar_ref, oT_ref, sel_ref, m_ref, l_ref, pv_ref,
                      sa_ref, sb_ref, *, nb):
    qi = pl.program_id(2)
    tq = qT_ref.shape[1]
    q2 = jnp.concatenate(_head_masked(qT_ref[...]), axis=1)
    qs = q2 * jnp.asarray(HEAD_DIM ** -0.5, BF16)
    nidx = lax.broadcasted_iota(jnp.int32, (nb, 2 * tq), 0)
    sc = jnp.dot(kbar_ref[...].astype(BF16), q2, preferred_element_type=F32)
    sc = jnp.where(nidx < qi, sc, -jnp.inf)
    rank = jnp.zeros((nb, 2 * tq), F32)
    for m in range(nb):
        r = sc[m:m + 1, :]
        beats = (r > sc) | ((r == sc) & (m < nidx))
        rank = rank + jnp.where(beats, 1.0, 0.0)
    sel_ref[...] = jnp.where((nidx < qi) & (rank < MOBA_TOPK), 0.0, NEG)

    def key_block(n):
        return k_ref[pl.ds(pl.multiple_of(n * MOBA_BLOCK, MOBA_BLOCK), MOBA_BLOCK), :]

    def past_logits(e):
        n = e - 1
        return jnp.dot(key_block(n), qs, preferred_element_type=F32) + sel_ref[pl.ds(n, 1), :]

    def reduce_into(e, s_ref, v_idx):
        s = s_ref[...]
        m = jnp.max(s, axis=0, keepdims=True)
        p = jnp.exp(s - m)
        pb = p.astype(BF16)
        vn = vT_ref[v_idx]
        row = pl.ds(e, 1)
        m_ref[row, :] = m
        l_ref[row, :] = jnp.sum(p, axis=0, keepdims=True)
        pv_ref[e] = jnp.concatenate(
            [jnp.dot(vn[:HEAD_DIM, :], pb[:, :tq], preferred_element_type=F32),
             jnp.dot(vn[HEAD_DIM:, :], pb[:, tq:], preferred_element_type=F32)], axis=1)

    kpos = lax.broadcasted_iota(jnp.int32, (MOBA_BLOCK, 2 * tq), 0)
    qpos = lax.broadcasted_iota(jnp.int32, (MOBA_BLOCK, 2 * tq), 1) % tq
    sa_ref[...] = (jnp.dot(key_block(qi), qs, preferred_element_type=F32)
                   + jnp.where(kpos <= qpos, 0.0, NEG))

    def pair(g, carry):
        e = 2 * g
        sb_ref[...] = past_logits(e + 1)
        reduce_into(e, sa_ref, jnp.where(e == 0, qi, e - 1))
        sa_ref[...] = past_logits(e + 2)
        reduce_into(e + 1, sb_ref, e)
        return carry

    n_pairs = qi // 2 + 1
    lax.fori_loop(0, n_pairs, pair, 0)
    n_elem = 2 * n_pairs

    m_all = lax.fori_loop(1, n_elem, lambda e, m: jnp.maximum(m, m_ref[pl.ds(e, 1), :]), m_ref[0:1, :])

    def merge(e, c):
        w = jnp.exp(m_ref[pl.ds(e, 1), :] - m_all)
        return c[0] + w * l_ref[pl.ds(e, 1), :], c[1] + w * pv_ref[e]

    l, acc = lax.fori_loop(0, n_elem, merge,
                           (jnp.zeros((1, 2 * tq), F32), jnp.zeros((HEAD_DIM, 2 * tq), F32)))
    o = acc / l
    oT_ref[...] = jnp.concatenate([o[:, :tq], o[:, tq:]], axis=0).astype(BF16)


def _sb_prompt_body(qT_ref, k_ref, vT_ref, oT_ref):
    qi = pl.program_id(2)
    tq = qT_ref.shape[1]
    qs = jnp.concatenate(_head_masked(qT_ref[...]), axis=1) * jnp.asarray(HEAD_DIM ** -0.5, BF16)
    kk = lax.broadcasted_iota(jnp.int32, (MOBA_BLOCK, MOBA_BLOCK), 0)
    jj = lax.broadcasted_iota(jnp.int32, (MOBA_BLOCK, MOBA_BLOCK), 1)
    later = jnp.where(kk < jj, 1.0, 0.0).astype(BF16)
    kpos = lax.broadcasted_iota(jnp.int32, (MOBA_BLOCK, 2 * tq), 0)
    qpos = lax.broadcasted_iota(jnp.int32, (MOBA_BLOCK, 2 * tq), 1) % tq

    def block_terms(n, visible):
        start = pl.multiple_of(n * MOBA_BLOCK, MOBA_BLOCK)
        z = jnp.dot(k_ref[pl.ds(start, MOBA_BLOCK), :], qs, preferred_element_type=F32)
        l, ls = _log_sigmoid_pair(z)
        if visible is not None:
            l = jnp.where(visible, l, 0.0)
        r = jnp.dot(later, l.astype(BF16), preferred_element_type=F32)
        a = jnp.exp(ls + r)
        if visible is not None:
            a = jnp.where(visible, a, 0.0)
        ab = a.astype(BF16)
        vn = vT_ref[n]
        pv = jnp.concatenate(
            [jnp.dot(vn[:HEAD_DIM, :], ab[:, :tq], preferred_element_type=F32),
             jnp.dot(vn[HEAD_DIM:, :], ab[:, tq:], preferred_element_type=F32)], axis=1)
        return pv, jnp.sum(l, axis=0, keepdims=True)

    pv_d, tot_d = block_terms(qi, kpos < qpos)
    pv_p, tot_p = block_terms(jnp.maximum(qi - 1, 0), None)
    has_prev = qi > 0
    acc = pv_d + jnp.where(has_prev, jnp.exp(tot_d), 0.0) * pv_p
    lsurv = tot_d + jnp.where(has_prev, tot_p, 0.0)

    def alive_flag(ls):
        return (jnp.max(ls) > LOG_F32_TINY).astype(jnp.int32)

    def cond(c):
        return jnp.logical_and(c[0] >= 0, c[3] > 0)

    def body(c):
        n, ls, acc, _ = c
        pv, tot = block_terms(n, None)
        ls_new = ls + tot
        return n - 1, ls_new, acc + jnp.exp(ls) * pv, alive_flag(ls_new)

    fin = lax.while_loop(cond, body, (qi - 2, lsurv, acc, alive_flag(lsurv)))
    o = fin[2]
    oT_ref[...] = jnp.concatenate([o[:, :tq], o[:, tq:]], axis=0).astype(BF16)


def _prompt_mixer_specs(seq, nb):
    tq = TOKEN_TILE
    q_spec = pl.BlockSpec((None, 2 * HEAD_DIM, tq), lambda b, hp, qi: (b, hp, qi))
    k_spec = pl.BlockSpec((None, seq, 2 * HEAD_DIM), lambda b, hp, qi: (b, 0, hp))
    v_spec = pl.BlockSpec((None, nb, 2 * HEAD_DIM, MOBA_BLOCK), lambda b, hp, qi: (b, 0, hp, 0))
    return q_spec, k_spec, v_spec


def _moba_prompt(qT, k_nat, vT_blk, kbar):
    batch, _, seq = qT.shape
    nb = seq // MOBA_BLOCK
    q_spec, k_spec, v_spec = _prompt_mixer_specs(seq, nb)
    return pl.pallas_call(
        functools.partial(_moba_prompt_body, nb=nb),
        out_shape=jax.ShapeDtypeStruct(qT.shape, BF16),
        grid=(batch, N_HEADS // 2, seq // TOKEN_TILE),
        in_specs=[q_spec, k_spec, v_spec,
                  pl.BlockSpec((None, nb, 2 * HEAD_DIM), lambda b, hp, qi: (b, 0, hp))],
        out_specs=q_spec,
        scratch_shapes=[pltpu.VMEM((nb, 2 * TOKEN_TILE), F32)] * 3
        + [pltpu.VMEM((nb, HEAD_DIM, 2 * TOKEN_TILE), F32)]
        + [pltpu.VMEM((MOBA_BLOCK, 2 * TOKEN_TILE), F32)] * 2,
        compiler_params=_cparams(("parallel", "parallel", "arbitrary")),
        name="moba_prompt",
    )(qT, k_nat, vT_blk, kbar)


def _sb_prompt(qT, k_nat, vT_blk):
    batch, _, seq = qT.shape
    nb = seq // MOBA_BLOCK
    q_spec, k_spec, v_spec = _prompt_mixer_specs(seq, nb)
    return pl.pallas_call(
        _sb_prompt_body,
        out_shape=jax.ShapeDtypeStruct(qT.shape, BF16),
        grid=(batch, N_HEADS // 2, seq // TOKEN_TILE),
        in_specs=[q_spec, k_spec, v_spec],
        out_specs=q_spec,
        compiler_params=_cparams(("parallel", "parallel", "arbitrary")),
        name="sb_prompt",
    )(qT, k_nat, vT_blk)


def _mem_attend_body(q_ref, k_ref, v_ref, o_ref):
    q = q_ref[...]
    k = k_ref[...].astype(BF16)
    v = v_ref[...].astype(BF16)
    scale = HEAD_DIM_MEM ** -0.5
    outs = []
    for h in range(N_HEADS_MEM):
        sl = slice(h * HEAD_DIM_MEM, (h + 1) * HEAD_DIM_MEM)
        s = lax.dot_general(q[:, sl], k[:, sl], _NT, preferred_element_type=F32) * scale
        p = jnp.exp(s - jnp.max(s, axis=1, keepdims=True))
        o = jnp.dot(p.astype(BF16), v[:, sl], preferred_element_type=F32)
        outs.append(o / jnp.sum(p, axis=1, keepdims=True))
    o_ref[...] = jnp.concatenate(outs, axis=1).astype(BF16)


def _mem_attend(q, k, v, tq):
    batch, seq, w = q.shape
    n_mem = k.shape[1]
    kv_spec = pl.BlockSpec((None, n_mem, w), lambda b, i: (b, 0, 0))
    q_spec = pl.BlockSpec((None, tq, w), lambda b, i: (b, i, 0))
    return pl.pallas_call(
        _mem_attend_body,
        out_shape=jax.ShapeDtypeStruct(q.shape, BF16),
        grid=(batch, seq // tq),
        in_specs=[q_spec, kv_spec, kv_spec],
        out_specs=q_spec,
        compiler_params=_cparams(("parallel", "parallel")),
        name="mem_attend",
    )(q, k, v)


def _merge_body(x_ref, oa_ref, ob_ref, om_ref, gt_ref, wa_ref, wb_ref, wm_ref, wo_ref, o_ref):
    merged = None
    for i, (o, w) in enumerate(((oa_ref, wa_ref), (ob_ref, wb_ref), (om_ref, wm_ref))):
        br = jnp.dot(o[...], w[...], preferred_element_type=F32)
        term = gt_ref[:, i * D_MODEL:(i + 1) * D_MODEL].astype(F32) * br
        merged = term if merged is None else merged + term
    o_ref[...] = x_ref[...] + jnp.dot(merged.astype(BF16), wo_ref[...], preferred_element_type=F32)


def _merge(x, oa, ob, om, gates, wa, wb, wm, wo, tm):
    n, d = x.shape
    row = lambda w: pl.BlockSpec((tm, w), lambda i: (i, 0))
    full = lambda a: pl.BlockSpec(a.shape, lambda i: (0, 0))
    return pl.pallas_call(
        _merge_body,
        out_shape=jax.ShapeDtypeStruct((n, d), F32),
        grid=(n // tm,),
        in_specs=[row(d), row(W_ATT), row(W_ATT), row(W_MEM), row(N_BRANCH * d),
                  full(wa), full(wb), full(wm), full(wo)],
        out_specs=row(d),
        compiler_params=_cparams(("parallel",)),
        name="merge",
    )(x, oa, ob, om, gates, wa, wb, wm, wo)


SELECT_PAGES_PER_STEP = 16


def _own_head_columns(acc, n_tok):
    lane_head = lax.broadcasted_iota(jnp.int32, (n_tok, W_ATT), 1) // HEAD_DIM
    out = jnp.zeros((n_tok, W_ATT), F32)
    for h in range(N_HEADS):
        out = out + jnp.where(lane_head == h, acc[h * n_tok:(h + 1) * n_tok, :], 0.0)
    return out


def _suffix_sum_lanes(x):
    lane = lax.broadcasted_iota(jnp.int32, x.shape, 1)
    n = x.shape[1]
    sh = 1
    while sh < n:
        x = x + jnp.where(lane < n - sh, pltpu.roll(x, n - sh, 1), 0.0)
        sh *= 2
    return x


SB_RING = 4


def _sb_decode_body(pt_ref, qbd_ref, knT_ref, vnT_ref, ck_hbm, cv_hbm, o_ref, kbuf, vbuf, sem,
                    *, n_tok, n_pages):
    b = pl.program_id(0)
    qbd = qbd_ref[...] * jnp.asarray(HEAD_DIM ** -0.5, BF16)
    rows = N_HEADS * n_tok
    ahead = SB_RING - 1

    def page_copies(p):
        slot = p % SB_RING
        phys = pt_ref[b, n_pages - 1 - p]
        return (pltpu.make_async_copy(ck_hbm.at[phys], kbuf.at[slot], sem.at[0, slot]),
                pltpu.make_async_copy(cv_hbm.at[phys], vbuf.at[slot], sem.at[1, slot]))

    def start(p):
        for c in page_copies(p):
            c.start()

    def wait(p):
        for c in page_copies(p):
            c.wait()

    def page(kT, vT, lsurv, visible):
        z = jnp.dot(qbd, kT.astype(BF16), preferred_element_type=F32)
        l, ls = _log_sigmoid_pair(z)
        if visible is not None:
            l = jnp.where(visible, l, 0.0)
        suffix = _suffix_sum_lanes(l)
        a = jnp.exp(ls + (suffix - l) + lsurv)
        if visible is not None:
            a = jnp.where(visible, a, 0.0)
        o = lax.dot_general(a.astype(BF16), vT.astype(BF16), _NT, preferred_element_type=F32)
        return o, lsurv + suffix[:, 0:1]

    def alive_flag(lsurv):
        return (jnp.max(lsurv) > LOG_F32_TINY).astype(jnp.int32)

    for p in range(ahead):
        start(p)

    key = lax.broadcasted_iota(jnp.int32, (rows, PAGE_SIZE), 1)
    tok = lax.broadcasted_iota(jnp.int32, (rows, PAGE_SIZE), 0) % n_tok
    acc0, ls0 = page(knT_ref[...], vnT_ref[...], jnp.zeros((rows, 1), F32), key < tok)

    def cond(c):
        return jnp.logical_and(c[0] < n_pages, c[3] > 0)

    def body(c):
        p, lsurv, acc, _ = c
        wait(p)

        @pl.when(p + ahead < n_pages)
        def _():
            start(p + ahead)

        slot = p % SB_RING
        o, lsurv = page(kbuf[slot], vbuf[slot], lsurv, None)
        return p + 1, lsurv, acc + o, alive_flag(lsurv)

    p_end, _, acc, _ = lax.while_loop(cond, body, (0, ls0, acc0, alive_flag(ls0)))

    for j in range(ahead):
        @pl.when(p_end + j < n_pages)
        def _():
            wait(p_end + j)

    o_ref[...] = _own_head_columns(acc, n_tok).astype(BF16)


def _sb_decode(page_table, qbd, knT, vnT, ckT, cvT, n_tok):
    batch, n_pages = page_table.shape
    assert n_pages >= SB_RING
    rows = N_HEADS * n_tok
    per_b = lambda r, c: pl.BlockSpec((None, r, c), lambda b, pt: (b, 0, 0))
    hbm = pl.BlockSpec(memory_space=pl.ANY)
    return pl.pallas_call(
        functools.partial(_sb_decode_body, n_tok=n_tok, n_pages=n_pages),
        out_shape=jax.ShapeDtypeStruct((batch, n_tok, W_ATT), BF16),
        grid_spec=pltpu.PrefetchScalarGridSpec(
            num_scalar_prefetch=1,
            grid=(batch,),
            in_specs=[per_b(rows, W_ATT), per_b(W_ATT, PAGE_SIZE), per_b(W_ATT, PAGE_SIZE), hbm, hbm],
            out_specs=per_b(n_tok, W_ATT),
            scratch_shapes=[pltpu.VMEM((SB_RING, W_ATT, PAGE_SIZE), F32),
                            pltpu.VMEM((SB_RING, W_ATT, PAGE_SIZE), F32),
                            pltpu.SemaphoreType.DMA((2, SB_RING))],
        ),
        compiler_params=_cparams(("arbitrary",)),
        name="sb_decode",
    )(page_table, qbd, knT, vnT, ckT, cvT)


def _moba_select_body(pt_ref, qbd_ref, *refs, n_pages):
    g = SELECT_PAGES_PER_STEP
    k_refs = refs[:g]
    idx_ref, sc_ref = refs[g:]
    step = pl.program_id(1)
    ppb = MOBA_BLOCK // PAGE_SIZE
    qbd = qbd_ref[...]
    rows = qbd.shape[0]
    lane = lax.broadcasted_iota(jnp.int32, (rows, LANES), 1)

    @pl.when(step == 0)
    def _():
        sc_ref[...] = jnp.zeros_like(sc_ref)

    sc = sc_ref[...]
    for i in range(0, g, ppb):
        ksum = k_refs[i][...]
        for j in range(1, ppb):
            ksum = ksum + k_refs[i + j][...]
        z = jnp.dot(qbd, ksum.astype(BF16), preferred_element_type=F32)
        blk = (step * g + i) // ppb
        sc = sc + jnp.where(lane == blk, jnp.sum(z, axis=1, keepdims=True), 0.0)
    sc_ref[...] = sc

    @pl.when(step == n_pages // g - 1)
    def _():
        nb = n_pages // ppb
        s = jnp.where(lane < nb, sc_ref[...], -jnp.inf)
        out = jnp.zeros((rows, LANES), jnp.int32)
        for j in range(MOBA_TOPK):
            best = jnp.max(s, axis=1, keepdims=True)
            pick = jnp.min(jnp.where(s == best, lane, LANES), axis=1, keepdims=True)
            out = jnp.where(lane == j, pick, out)
            s = jnp.where(lane == pick, -jnp.inf, s)
        idx_ref[...] = out


def _moba_select(page_table, qbd, ckT):
    batch, n_pages = page_table.shape
    g = SELECT_PAGES_PER_STEP
    rows = qbd.shape[1]

    def page_spec(i):
        return pl.BlockSpec((None, W_ATT, PAGE_SIZE), lambda b, s, pt: (pt[b, s * g + i], 0, 0))

    return pl.pallas_call(
        functools.partial(_moba_select_body, n_pages=n_pages),
        out_shape=jax.ShapeDtypeStruct((batch, rows, LANES), jnp.int32),
        grid_spec=pltpu.PrefetchScalarGridSpec(
            num_scalar_prefetch=1,
            grid=(batch, n_pages // g),
            in_specs=[pl.BlockSpec((None, rows, W_ATT), lambda b, s, pt: (b, 0, 0))]
            + [page_spec(i) for i in range(g)],
            out_specs=pl.BlockSpec((None, rows, LANES), lambda b, s, pt: (b, 0, 0)),
            scratch_shapes=[pltpu.VMEM((rows, LANES), F32)],
        ),
        compiler_params=_cparams(("parallel", "arbitrary")),
        name="moba_select",
    )(page_table, qbd, *([ckT] * g))


def _moba_gather_body(pt_ref, idx_ref, q_ref, knT_ref, vnT_ref, ck_hbm, cv_hbm, o_ref,
                      kbuf, vbuf, sem, *, n_tok, n_pages):
    ppb = MOBA_BLOCK // PAGE_SIZE
    n_slots = n_tok * MOBA_TOPK
    scale = HEAD_DIM ** -0.5
    step = pl.program_id(0) * N_HEADS + pl.program_id(1)
    n_steps = pl.num_programs(0) * N_HEADS
    half = step % 2

    def copies(st, hf, lookup):
        bb, hh = st // N_HEADS, st % N_HEADS
        out = []
        for slot in range(n_slots):
            tok, j = divmod(slot, MOBA_TOPK)
            blk = idx_ref[((bb * N_HEADS + hh) * n_tok + tok) * MOBA_TOPK + j] if lookup else 0
            for i in range(ppb):
                pg = pt_ref[bb * n_pages + blk * ppb + i] if lookup else 0
                dst = pl.ds(i * PAGE_SIZE, PAGE_SIZE)
                out.append(pltpu.make_async_copy(ck_hbm.at[pg, hh], kbuf.at[hf, slot, :, dst], sem.at[0, hf]))
                out.append(pltpu.make_async_copy(cv_hbm.at[pg, hh], vbuf.at[hf, slot, :, dst], sem.at[1, hf]))
        return out

    @pl.when(step == 0)
    def _():
        for c in copies(step, half, True):
            c.start()

    @pl.when(step + 1 < n_steps)
    def _():
        for c in copies(step + 1, 1 - half, True):
            c.start()

    for c in copies(step, half, False):
        c.wait()

    kbuf = kbuf.at[half]
    vbuf = vbuf.at[half]
    q = q_ref[...]
    tok_row = lax.broadcasted_iota(jnp.int32, (n_tok, MOBA_BLOCK), 0)
    key = lax.broadcasted_iota(jnp.int32, (n_tok, PAGE_SIZE), 1)
    tok = lax.broadcasted_iota(jnp.int32, (n_tok, PAGE_SIZE), 0)
    s_own = jnp.dot(q, knT_ref[...], preferred_element_type=F32) * scale + jnp.where(key <= tok, 0.0, NEG)
    logits = []
    for slot in range(n_slots):
        s = jnp.dot(q, kbuf[slot].astype(BF16), preferred_element_type=F32) * scale
        logits.append(s + jnp.where(tok_row == slot // MOBA_TOPK, 0.0, NEG))
    m = jnp.max(s_own, axis=1, keepdims=True)
    for s in logits:
        m = jnp.maximum(m, jnp.max(s, axis=1, keepdims=True))
    p_own = jnp.exp(s_own - m)
    den = jnp.sum(p_own, axis=1, keepdims=True)
    acc = lax.dot_general(p_own.astype(BF16), vnT_ref[...], _NT, preferred_element_type=F32)
    for slot, s in enumerate(logits):
        p = jnp.exp(s - m)
        den = den + jnp.sum(p, axis=1, keepdims=True)
        acc = acc + lax.dot_general(p.astype(BF16), vbuf[slot].astype(BF16), _NT, preferred_element_type=F32)
    o_ref[...] = (acc / den).astype(BF16)


def _moba_gather(pt_flat, idx_flat, q, knT, vnT, ckT, cvT, n_pages):
    batch, _, n_tok, _ = q.shape
    n_slots = n_tok * MOBA_TOPK
    per_bh = lambda r, c: pl.BlockSpec((None, None, r, c), lambda b, h, pt, ix: (b, h, 0, 0))
    return pl.pallas_call(
        functools.partial(_moba_gather_body, n_tok=n_tok, n_pages=n_pages),
        out_shape=jax.ShapeDtypeStruct((batch, N_HEADS, n_tok, HEAD_DIM), BF16),
        grid_spec=pltpu.PrefetchScalarGridSpec(
            num_scalar_prefetch=2,
            grid=(batch, N_HEADS),
            in_specs=[per_bh(n_tok, HEAD_DIM), per_bh(HEAD_DIM, PAGE_SIZE), per_bh(HEAD_DIM, PAGE_SIZE),
                      pl.BlockSpec(memory_space=pl.ANY), pl.BlockSpec(memory_space=pl.ANY)],
            out_specs=per_bh(n_tok, HEAD_DIM),
            scratch_shapes=[pltpu.VMEM((2, n_slots, HEAD_DIM, MOBA_BLOCK), F32),
                            pltpu.VMEM((2, n_slots, HEAD_DIM, MOBA_BLOCK), F32),
                            pltpu.SemaphoreType.DMA((2, 2))],
        ),
        compiler_params=_cparams(("arbitrary", "arbitrary")),
        name="moba_gather",
    )(pt_flat, idx_flat, q, knT, vnT, ckT, cvT)


def _block_diag_q(q, n_tok):
    batch = q.shape[0] // n_tok
    q3 = q.reshape(batch, 1, n_tok, W_ATT)
    col_head = (jnp.arange(W_ATT) // HEAD_DIM)[None, None, None, :]
    row_head = jnp.arange(N_HEADS)[None, :, None, None]
    return jnp.where(col_head == row_head, q3, 0.0).reshape(batch, N_HEADS * n_tok, W_ATT)


def _new_tokens_T(x, n_tok):
    batch = x.shape[0] // n_tok
    xt = x.reshape(batch, n_tok, W_ATT).transpose(0, 2, 1)
    return jnp.pad(xt, ((0, 0), (0, 0), (0, PAGE_SIZE - n_tok)))


def _cache_T(c):
    return c.transpose(0, 2, 3, 1)


def kernel(x_prompt, x_sample, cache_k_moba, cache_v_moba, cache_k_sb, cache_v_sb, cache_mem_k, cache_mem_v,
           page_table, mem_prompt, norm_ffn1, w_ffn1_gu, w_ffn1_down, norm_mix, w_in, norm_mem, w_mem_kv,
           w_br_moba, w_br_sb, w_br_mem, w_out, norm_ffn2, w_ffn2_gu, w_ffn2_down, norm_final):
    batch, seq, d = x_prompt.shape
    dec_batch, n_tok, _ = x_sample.shape
    n_pages = page_table.shape[1]
    past = n_pages * PAGE_SIZE
    depth = w_in.shape[0]
    assert depth == 1 and past % MOBA_BLOCK == 0 and seq % TOKEN_TILE == 0

    lyr = 0
    bf = lambda w: w[lyr].astype(BF16)
    row = lambda g: g.reshape(1, d)
    w1gu, w1d, w2gu, w2d = bf(w_ffn1_gu), bf(w_ffn1_down), bf(w_ffn2_gu), bf(w_ffn2_down)
    win, wmem, wa, wb, wm, wo = bf(w_in), bf(w_mem_kv), bf(w_br_moba), bf(w_br_sb), bf(w_br_mem), bf(w_out)
    n1, nmix, nmem, n2, nfin = row(norm_ffn1[lyr]), row(norm_mix[lyr]), row(norm_mem[lyr]), row(norm_ffn2[lyr]), row(norm_final)

    n = batch * seq
    nb = seq // MOBA_BLOCK
    xp = _ffn(x_prompt.reshape(n, d), n1, w1gu, w1d, nfin, final_norm=False, tm=1024)
    (qaT, kaT, vaT, ka_nat, vaT_blk, kbar, qbT, kbT, vbT, kb_nat, vbT_blk, qm, gates) = _proj_prompt(
        xp, nmix, win, _rope_tables(jnp.arange(seq)), batch, seq)
    mem_kv = _normproj(mem_prompt.reshape(-1, d), nmem, wmem, tm=512)
    n_mem = mem_prompt.shape[1]
    mk = mem_kv[:, :W_MEM].reshape(batch, n_mem, W_MEM)
    mv = mem_kv[:, W_MEM:].reshape(batch, n_mem, W_MEM)
    oaT = _moba_prompt(qaT, ka_nat.reshape(batch, seq, W_ATT), vaT_blk, kbar.reshape(batch, nb, W_ATT))
    obT = _sb_prompt(qbT, kb_nat.reshape(batch, seq, W_ATT), vbT_blk)
    om = _mem_attend(qm.reshape(batch, seq, W_MEM), mk, mv, tq=512)
    nat = lambda t: t.transpose(0, 2, 1).reshape(n, W_ATT)
    xp = _merge(xp, nat(oaT), nat(obT), om.reshape(n, W_MEM), gates, wa, wb, wm, wo, tm=512)
    y_prompt = _ffn(xp, n2, w2gu, w2d, nfin, final_norm=True, tm=1024).reshape(batch, seq, d)
    rows_out = lambda t: t.reshape(batch, N_HEADS, HEAD_DIM, seq).transpose(0, 3, 1, 2)[None]
    mem_out = lambda t: t.reshape(1, batch, n_mem, N_HEADS_MEM, HEAD_DIM_MEM)

    ns = dec_batch * n_tok
    xs = _ffn(x_sample.reshape(ns, d), n1, w1gu, w1d, nfin, final_norm=False, tm=ns)
    pos_s = jnp.tile(past + jnp.arange(n_tok), dec_batch)
    qa, ka, va, qb, kb, vb, qm_s, gates_s = _proj_sample(xs, nmix, win, _rope_tables(pos_s))

    ck_a, cv_a = _cache_T(cache_k_moba[lyr]), _cache_T(cache_v_moba[lyr])
    ck_b, cv_b = _cache_T(cache_k_sb[lyr]), _cache_T(cache_v_sb[lyr])
    pool = ck_a.shape[0]
    flat = lambda c: c.reshape(pool, W_ATT, PAGE_SIZE)

    idx = _moba_select(page_table, _block_diag_q(qa, n_tok).astype(BF16), flat(ck_a))
    heads = lambda t: t.reshape(dec_batch, n_tok, N_HEADS, HEAD_DIM).transpose(0, 2, 1, 3)
    heads_T = lambda t: _new_tokens_T(t, n_tok).reshape(dec_batch, N_HEADS, HEAD_DIM, PAGE_SIZE)
    oa_s = _moba_gather(page_table.reshape(-1), idx[:, :, :MOBA_TOPK].reshape(-1),
                        heads(qa).astype(BF16), heads_T(ka).astype(BF16), heads_T(va).astype(BF16),
                        ck_a, cv_a, n_pages)
    oa_s = oa_s.transpose(0, 2, 1, 3).reshape(ns, W_ATT)
    ob_s = _sb_decode(page_table, _block_diag_q(qb, n_tok).astype(BF16),
                      _new_tokens_T(kb, n_tok).astype(BF16), _new_tokens_T(vb, n_tok).astype(BF16),
                      flat(ck_b), flat(cv_b), n_tok).reshape(ns, W_ATT)
    om_s = _mem_attend(qm_s.reshape(dec_batch, n_tok, W_MEM),
                       cache_mem_k[lyr].reshape(dec_batch, -1, W_MEM),
                       cache_mem_v[lyr].reshape(dec_batch, -1, W_MEM), tq=n_tok).reshape(ns, W_MEM)
    xs = _merge(xs, oa_s, ob_s, om_s, gates_s, wa, wb, wm, wo, tm=ns)
    y_sample = _ffn(xs, n2, w2gu, w2d, nfin, final_norm=True, tm=ns).reshape(dec_batch, n_tok, d)
    new_rows = lambda t: t.reshape(1, dec_batch, n_tok, N_HEADS, HEAD_DIM)

    return (y_prompt, y_sample, rows_out(kaT), rows_out(vaT), rows_out(kbT), rows_out(vbT),
            mem_out(mk), mem_out(mv), new_rows(ka), new_rows(va), new_rows(kb), new_rows(vb))
```

```python
import functools

import jax
import jax.numpy as jnp
from jax import lax
from jax.experimental import pallas as pl
from jax.experimental.pallas import tpu as pltpu

F32 = jnp.float32
BF16 = jnp.bfloat16

D_MODEL = 1024
HEAD_DIM = 64
N_HEADS = 8
W_ATT = N_HEADS * HEAD_DIM
N_HEADS_MEM = 4
HEAD_DIM_MEM = 128
W_MEM = N_HEADS_MEM * HEAD_DIM_MEM
N_BRANCH = 3
D_IN = 6 * W_ATT + W_MEM + N_BRANCH * D_MODEL
D_FF = 2816
MOBA_BLOCK = 256
MOBA_TOPK = 3
PAGE_SIZE = 128
ROT_DIM = HEAD_DIM // 4
ROPE_THETA = 500000.0
EPS = 1e-6

NEG = -1e30
LOG_F32_TINY = -88.0
LOG2_E = 1.4426950408889634
FF_CHUNK = 256
TOKEN_TILE = 256
LANES = 128
VMEM_LIMIT_BYTES = 56 * 1024 * 1024

_NT = (((1,), (1,)), ((), ()))


def _cparams(sem):
    return pltpu.CompilerParams(dimension_semantics=sem, vmem_limit_bytes=VMEM_LIMIT_BYTES)


def _rms(x, g):
    return x * lax.rsqrt(jnp.mean(x * x, axis=-1, keepdims=True) + EPS) * g


def _softplus_pair(z):
    t = jnp.log(1.0 + jnp.exp2(jnp.abs(z) * -LOG2_E))
    return jnp.maximum(z, 0.0) + t, jnp.minimum(z, 0.0) - t


def _log_sigmoid_pair(z):
    sp, lsig = _softplus_pair(z)
    return -sp, lsig


def _ffn_body(x_ref, g_ref, wg_ref, wu_ref, wd_ref, gf_ref, o_ref, h_ref, acc_ref, *, final_norm):
    c = pl.program_id(1)

    @pl.when(c == 0)
    def _():
        h_ref[...] = _rms(x_ref[...], g_ref[...]).astype(BF16)
        acc_ref[...] = jnp.zeros_like(acc_ref)

    h = h_ref[...]
    gate = jnp.dot(h, wg_ref[...], preferred_element_type=F32)
    up = jnp.dot(h, wu_ref[...], preferred_element_type=F32)
    act = (gate * jax.nn.sigmoid(gate) * up).astype(BF16)
    acc_ref[...] += jnp.dot(act, wd_ref[...], preferred_element_type=F32)

    @pl.when(c == pl.num_programs(1) - 1)
    def _():
        y = x_ref[...] + 0.5 * acc_ref[...]
        if final_norm:
            y = _rms(y, gf_ref[...])
        o_ref[...] = y


def _ffn(x, g, w_gu, w_down, g_final, *, final_norm, tm):
    n, d = x.shape
    nc = D_FF // FF_CHUNK
    return pl.pallas_call(
        functools.partial(_ffn_body, final_norm=final_norm),
        out_shape=jax.ShapeDtypeStruct((n, d), F32),
        grid=(n // tm, nc),
        in_specs=[
            pl.BlockSpec((tm, d), lambda i, c: (i, 0)),
            pl.BlockSpec((1, d), lambda i, c: (0, 0)),
            pl.BlockSpec((d, FF_CHUNK), lambda i, c: (0, c)),
            pl.BlockSpec((d, FF_CHUNK), lambda i, c: (0, nc + c)),
            pl.BlockSpec((FF_CHUNK, d), lambda i, c: (c, 0)),
            pl.BlockSpec((1, d), lambda i, c: (0, 0)),
        ],
        out_specs=pl.BlockSpec((tm, d), lambda i, c: (i, 0)),
        scratch_shapes=[pltpu.VMEM((tm, d), BF16), pltpu.VMEM((tm, d), F32)],
        compiler_params=_cparams(("parallel", "arbitrary")),
        name="ffn",
    )(x, g, w_gu, w_gu, w_down, g_final)


def _rope_tables(pos):
    half = ROT_DIM // 2
    inv = ROPE_THETA ** (-(jnp.arange(half, dtype=F32) * 2.0) / ROT_DIM)
    ang = pos.astype(F32)[:, None] * inv[None, :]
    cos, sin = jnp.cos(ang), jnp.sin(ang)
    t = pos.shape[0]
    rest = HEAD_DIM - ROT_DIM
    z8 = jnp.zeros((t, half), F32)
    cos_h = jnp.concatenate([cos, cos, jnp.ones((t, rest), F32)], axis=1)
    sa_h = jnp.concatenate([-sin, z8, jnp.zeros((t, rest), F32)], axis=1)
    sb_h = jnp.concatenate([z8, sin, jnp.zeros((t, rest), F32)], axis=1)
    two = lambda a: jnp.concatenate([a, a], axis=1)
    return two(cos_h), two(sa_h), two(sb_h)


def _rope(z, cos, sa, sb):
    half = ROT_DIM // 2
    w = z.shape[1]
    return z * cos + pltpu.roll(z, w - half, 1) * sa + pltpu.roll(z, half, 1) * sb


def _proj_common(x_ref, g_ref, w_ref, cos_ref, sa_ref, sb_ref):
    h = _rms(x_ref[...], g_ref[...]).astype(BF16)
    reps = W_ATT // LANES
    wide = lambda r: jnp.concatenate([r[...]] * reps, axis=1)
    cos, sa, sb = wide(cos_ref), wide(sa_ref), wide(sb_ref)

    def col(c):
        return jnp.dot(h, w_ref[:, c * W_ATT:(c + 1) * W_ATT], preferred_element_type=F32)

    rope = lambda z: _rope(z, cos, sa, sb)
    return col, rope


def _store_gates(col, gt_ref):
    for c in range(N_BRANCH * D_MODEL // W_ATT):
        gt_ref[:, c * W_ATT:(c + 1) * W_ATT] = jax.nn.sigmoid(col(7 + c)).astype(BF16)


def _proj_prompt_body(x_ref, g_ref, w_ref, cos_ref, sa_ref, sb_ref,
                      qaT_ref, kaT_ref, vaT_ref, kan_ref, vaTb_ref, kbar_ref,
                      qbT_ref, kbT_ref, vbT_ref, kbn_ref, vbTb_ref, qm_ref, gt_ref):
    col, rope = _proj_common(x_ref, g_ref, w_ref, cos_ref, sa_ref, sb_ref)
    qaT_ref[...] = rope(col(0)).T.astype(BF16)
    ka = rope(col(1))
    kaT_ref[...] = ka.T
    kan_ref[...] = ka.astype(BF16)
    kbar_ref[...] = jnp.sum(ka, axis=0, keepdims=True) * (1.0 / MOBA_BLOCK)
    vaT = col(2).T
    vaT_ref[...] = vaT
    vaTb_ref[...] = vaT.astype(BF16)
    qbT_ref[...] = col(3).T.astype(BF16)
    kb = col(4)
    kbT_ref[...] = kb.T
    kbn_ref[...] = kb.astype(BF16)
    vbT = col(5).T
    vbT_ref[...] = vbT
    vbTb_ref[...] = vbT.astype(BF16)
    qm_ref[...] = col(6).astype(BF16)
    _store_gates(col, gt_ref)


def _proj_sample_body(x_ref, g_ref, w_ref, cos_ref, sa_ref, sb_ref,
                      qa_ref, ka_ref, va_ref, qb_ref, kb_ref, vb_ref, qm_ref, gt_ref):
    col, rope = _proj_common(x_ref, g_ref, w_ref, cos_ref, sa_ref, sb_ref)
    qa_ref[...] = rope(col(0))
    ka_ref[...] = rope(col(1))
    va_ref[...] = col(2)
    qb_ref[...] = col(3)
    kb_ref[...] = col(4)
    vb_ref[...] = col(5)
    qm_ref[...] = col(6).astype(BF16)
    _store_gates(col, gt_ref)


def _proj_in_specs(tm, n_tab_tiles):
    tab = pl.BlockSpec((tm, LANES), lambda i: (i % n_tab_tiles, 0))
    return [
        pl.BlockSpec((tm, D_MODEL), lambda i: (i, 0)),
        pl.BlockSpec((1, D_MODEL), lambda i: (0, 0)),
        pl.BlockSpec((D_MODEL, D_IN), lambda i: (0, 0), pipeline_mode=pl.Buffered(1)),
        tab, tab, tab,
    ]


def _proj_prompt(x, g, w_in, tables, batch, seq):
    tm = TOKEN_TILE
    nb = seq // tm
    n = batch * seq
    sds = jax.ShapeDtypeStruct
    t_spec = pl.BlockSpec((None, W_ATT, tm), lambda i: (i // nb, 0, i % nb))
    blk_spec = pl.BlockSpec((None, None, W_ATT, tm), lambda i: (i // nb, i % nb, 0, 0))
    nat_spec = pl.BlockSpec((tm, W_ATT), lambda i: (i, 0))
    t_bf = sds((batch, W_ATT, seq), BF16)
    t_f32 = sds((batch, W_ATT, seq), F32)
    blk_bf = sds((batch, nb, W_ATT, tm), BF16)
    nat_bf = sds((n, W_ATT), BF16)
    out_shape = [t_bf, t_f32, t_f32, nat_bf, blk_bf, sds((n // tm, 1, W_ATT), F32),
                 t_bf, t_f32, t_f32, nat_bf, blk_bf, nat_bf, sds((n, N_BRANCH * D_MODEL), BF16)]
    out_specs = [t_spec, t_spec, t_spec, nat_spec, blk_spec,
                 pl.BlockSpec((None, 1, W_ATT), lambda i: (i, 0, 0)),
                 t_spec, t_spec, t_spec, nat_spec, blk_spec, nat_spec,
                 pl.BlockSpec((tm, N_BRANCH * D_MODEL), lambda i: (i, 0))]
    return pl.pallas_call(
        _proj_prompt_body,
        out_shape=out_shape,
        grid=(n // tm,),
        in_specs=_proj_in_specs(tm, nb),
        out_specs=out_specs,
        compiler_params=_cparams(("parallel",)),
        name="proj_prompt",
    )(x, g, w_in, *tables)


def _proj_sample(x, g, w_in, tables):
    n = x.shape[0]
    sds = jax.ShapeDtypeStruct
    nat = lambda dt: sds((n, W_ATT), dt)
    spec = pl.BlockSpec((n, W_ATT), lambda i: (0, 0))
    out_shape = [nat(F32)] * 6 + [nat(BF16), sds((n, N_BRANCH * D_MODEL), BF16)]
    out_specs = [spec] * 7 + [pl.BlockSpec((n, N_BRANCH * D_MODEL), lambda i: (0, 0))]
    return pl.pallas_call(
        _proj_sample_body,
        out_shape=out_shape,
        grid=(1,),
        in_specs=_proj_in_specs(n, 1),
        out_specs=out_specs,
        compiler_params=_cparams(("arbitrary",)),
        name="proj_sample",
    )(x, g, w_in, *tables)


def _normproj_body(x_ref, g_ref, w_ref, o_ref):
    h = _rms(x_ref[...], g_ref[...]).astype(BF16)
    o_ref[...] = jnp.dot(h, w_ref[...], preferred_element_type=F32)


def _normproj(x, g, w, tm):
    n, d = x.shape
    dout = w.shape[1]
    return pl.pallas_call(
        _normproj_body,
        out_shape=jax.ShapeDtypeStruct((n, dout), F32),
        grid=(n // tm,),
        in_specs=[pl.BlockSpec((tm, d), lambda i: (i, 0)),
                  pl.BlockSpec((1, d), lambda i: (0, 0)),
                  pl.BlockSpec((d, dout), lambda i: (0, 0))],
        out_specs=pl.BlockSpec((tm, dout), lambda i: (i, 0)),
        compiler_params=_cparams(("parallel",)),
        name="mem_kv",
    )(x, g, w)


def _head_masked(qT):
    row = lax.broadcasted_iota(jnp.int32, qT.shape, 0)
    zero = jnp.zeros_like(qT)
    return [jnp.where(row < HEAD_DIM, qT, zero), jnp.where(row >= HEAD_DIM, qT, zero)]


MOBA_SUM_ROWS = 16


def _moba_prompt_body(qT_ref, k_ref, vT_ref, kbar_ref, oT_ref, sel_ref, m_ref, pv_ref,
                      sa_ref, sb_ref, pa_ref, pb_ref, *, nb):
    qi = pl.program_id(2)
    tq = qT_ref.shape[1]
    q2 = jnp.concatenate(_head_masked(qT_ref[...]), axis=1)
    qs = q2 * jnp.asarray(HEAD_DIM ** -0.5, BF16)
    nidx = lax.broadcasted_iota(jnp.int32, (nb, 2 * tq), 0)
    sc = jnp.dot(kbar_ref[...].astype(BF16), q2, preferred_element_type=F32)
    sc = jnp.where(nidx < qi, sc, -jnp.inf)
    rank = jnp.zeros((nb, 2 * tq), F32)
    for m in range(nb):
        r = sc[m:m + 1, :]
        beats = (r > sc) | ((r == sc) & (m < nidx))
        rank = rank + jnp.where(beats, 1.0, 0.0)
    sel_ref[...] = jnp.where((nidx < qi) & (rank < MOBA_TOPK), 0.0, NEG)

    def key_block(n):
        return k_ref[pl.ds(pl.multiple_of(n * MOBA_BLOCK, MOBA_BLOCK), MOBA_BLOCK), :]

    def past_logits(e):
        n = e - 1
        return jnp.dot(key_block(n), qs, preferred_element_type=F32) + sel_ref[pl.ds(n, 1), :]

    def weights_into(e, s_ref, p_ref):
        s = s_ref[...]
        m = jnp.max(s, axis=0, keepdims=True)
        m_ref[pl.ds(e, 1), :] = m
        p_ref[...] = jnp.exp(s - m).astype(BF16)

    ones_rows = jnp.ones((MOBA_SUM_ROWS, MOBA_BLOCK), BF16)

    def pv_into(slot, p_ref, v_idx):
        pb = p_ref[...]
        vn = vT_ref[v_idx]
        lhs = [jnp.concatenate([vn[hh * HEAD_DIM:(hh + 1) * HEAD_DIM, :], ones_rows], axis=0) for hh in range(2)]
        pv_ref[slot] = jnp.concatenate(
            [jnp.dot(lhs[0], pb[:, :tq], preferred_element_type=F32),
             jnp.dot(lhs[1], pb[:, tq:], preferred_element_type=F32)], axis=1)

    def value_block(e):
        return jnp.where(e == 0, qi, e - 1)

    kpos = lax.broadcasted_iota(jnp.int32, (MOBA_BLOCK, 2 * tq), 0)
    qpos = lax.broadcasted_iota(jnp.int32, (MOBA_BLOCK, 2 * tq), 1) % tq
    sa_ref[...] = (jnp.dot(key_block(qi), qs, preferred_element_type=F32)
                   + jnp.where(kpos <= qpos, 0.0, NEG))
    pb_ref[...] = jnp.zeros_like(pb_ref)
    m_ref[...] = jnp.full(m_ref.shape, NEG, F32)
    spare = nb

    def pair(g, carry):
        e = 2 * g
        pv_into(jnp.where(e == 0, spare, e - 1), pb_ref, jnp.where(e == 0, 0, value_block(e - 1)))
        sb_ref[...] = past_logits(e + 1)
        weights_into(e, sa_ref, pa_ref)
        pv_into(e, pa_ref, value_block(e))
        sa_ref[...] = past_logits(e + 2)
        weights_into(e + 1, sb_ref, pb_ref)
        return carry

    n_pairs = qi // 2 + 1
    lax.fori_loop(0, n_pairs, pair, 0)
    n_elem = 2 * n_pairs
    pv_into(n_elem - 1, pb_ref, n_elem - 2)

    m_all = jnp.max(m_ref[...], axis=0, keepdims=True)

    def merge(e, acc):
        return acc + jnp.exp(m_ref[pl.ds(e, 1), :] - m_all) * pv_ref[e]

    acc = lax.fori_loop(0, n_elem, merge, jnp.zeros((HEAD_DIM + MOBA_SUM_ROWS, 2 * tq), F32))
    o = acc[:HEAD_DIM, :] / acc[HEAD_DIM:HEAD_DIM + 1, :]
    oT_ref[...] = jnp.concatenate([o[:, :tq], o[:, tq:]], axis=0).astype(BF16)


def _sb_prompt_body(qT_ref, k_ref, vT_ref, oT_ref):
    qi = pl.program_id(2)
    tq = qT_ref.shape[1]
    qs = jnp.concatenate(_head_masked(qT_ref[...]), axis=1) * jnp.asarray(HEAD_DIM ** -0.5, BF16)
    kk = lax.broadcasted_iota(jnp.int32, (MOBA_BLOCK, MOBA_BLOCK), 0)
    jj = lax.broadcasted_iota(jnp.int32, (MOBA_BLOCK, MOBA_BLOCK), 1)
    neg_later = jnp.where(kk < jj, -1.0, 0.0).astype(BF16)
    kpos = lax.broadcasted_iota(jnp.int32, (MOBA_BLOCK, 2 * tq), 0)
    qpos = lax.broadcasted_iota(jnp.int32, (MOBA_BLOCK, 2 * tq), 1) % tq

    def blocks_terms(blocks):
        zs = [jnp.dot(k_ref[pl.ds(pl.multiple_of(n * MOBA_BLOCK, MOBA_BLOCK), MOBA_BLOCK), :], qs,
                      preferred_element_type=F32) for n, _ in blocks]
        pairs = [_softplus_pair(z) for z in zs]
        sps = [sp if vis is None else jnp.where(vis, sp, 0.0) for (sp, _), (_, vis) in zip(pairs, blocks)]
        rs = [jnp.dot(neg_later, sp.astype(BF16), preferred_element_type=F32) for sp in sps]
        out = []
        for (n, vis), (_, lsig), sp, r in zip(blocks, pairs, sps, rs):
            a = jnp.exp(lsig + r)
            if vis is not None:
                a = jnp.where(vis, a, 0.0)
            ab = a.astype(BF16)
            vn = vT_ref[n]
            pv = jnp.concatenate(
                [jnp.dot(vn[:HEAD_DIM, :], ab[:, :tq], preferred_element_type=F32),
                 jnp.dot(vn[HEAD_DIM:, :], ab[:, tq:], preferred_element_type=F32)], axis=1)
            out.append((pv, r[0:1, :] - sp[0:1, :]))
        return out

    (pv_d, tot_d), (pv_p, tot_p) = blocks_terms([(qi, kpos < qpos), (jnp.maximum(qi - 1, 0), None)])
    has_prev = qi > 0
    acc = pv_d + jnp.where(has_prev, jnp.exp(tot_d), 0.0) * pv_p
    lsurv = tot_d + jnp.where(has_prev, tot_p, 0.0)

    def alive_flag(ls):
        return (jnp.max(ls) > LOG_F32_TINY).astype(jnp.int32)

    def cond(c):
        return jnp.logical_and(c[0] >= 0, c[3] > 0)

    def body(c):
        n, ls, acc, _ = c
        (pv, tot), = blocks_terms([(n, None)])
        ls_new = ls + tot
        return n - 1, ls_new, acc + jnp.exp(ls) * pv, alive_flag(ls_new)

    fin = lax.while_loop(cond, body, (qi - 2, lsurv, acc, alive_flag(lsurv)))
    o = fin[2]
    oT_ref[...] = jnp.concatenate([o[:, :tq], o[:, tq:]], axis=0).astype(BF16)


def _prompt_mixer_specs(seq, nb):
    tq = TOKEN_TILE
    q_spec = pl.BlockSpec((None, 2 * HEAD_DIM, tq), lambda b, hp, qi: (b, hp, qi))
    k_spec = pl.BlockSpec((None, seq, 2 * HEAD_DIM), lambda b, hp, qi: (b, 0, hp))
    v_spec = pl.BlockSpec((None, nb, 2 * HEAD_DIM, MOBA_BLOCK), lambda b, hp, qi: (b, 0, hp, 0))
    return q_spec, k_spec, v_spec


def _moba_prompt(qT, k_nat, vT_blk, kbar):
    batch, _, seq = qT.shape
    nb = seq // MOBA_BLOCK
    q_spec, k_spec, v_spec = _prompt_mixer_specs(seq, nb)
    return pl.pallas_call(
        functools.partial(_moba_prompt_body, nb=nb),
        out_shape=jax.ShapeDtypeStruct(qT.shape, BF16),
        grid=(batch, N_HEADS // 2, seq // TOKEN_TILE),
        in_specs=[q_spec, k_spec, v_spec,
                  pl.BlockSpec((None, nb, 2 * HEAD_DIM), lambda b, hp, qi: (b, 0, hp))],
        out_specs=q_spec,
        scratch_shapes=[pltpu.VMEM((nb, 2 * TOKEN_TILE), F32)] * 2
        + [pltpu.VMEM((nb + 1, HEAD_DIM + MOBA_SUM_ROWS, 2 * TOKEN_TILE), F32)]
        + [pltpu.VMEM((MOBA_BLOCK, 2 * TOKEN_TILE), F32)] * 2
        + [pltpu.VMEM((MOBA_BLOCK, 2 * TOKEN_TILE), BF16)] * 2,
        compiler_params=_cparams(("parallel", "parallel", "arbitrary")),
        name="moba_prompt",
    )(qT, k_nat, vT_blk, kbar)


def _sb_prompt(qT, k_nat, vT_blk):
    batch, _, seq = qT.shape
    nb = seq // MOBA_BLOCK
    q_spec, k_spec, v_spec = _prompt_mixer_specs(seq, nb)
    return pl.pallas_call(
        _sb_prompt_body,
        out_shape=jax.ShapeDtypeStruct(qT.shape, BF16),
        grid=(batch, N_HEADS // 2, seq // TOKEN_TILE),
        in_specs=[q_spec, k_spec, v_spec],
        out_specs=q_spec,
        compiler_params=_cparams(("parallel", "parallel", "arbitrary")),
        name="sb_prompt",
    )(qT, k_nat, vT_blk)


def _mem_attend_body(q_ref, k_ref, v_ref, o_ref):
    q = q_ref[...]
    k = k_ref[...].astype(BF16)
    v = v_ref[...].astype(BF16)
    scale = HEAD_DIM_MEM ** -0.5
    outs = []
    for h in range(N_HEADS_MEM):
        sl = slice(h * HEAD_DIM_MEM, (h + 1) * HEAD_DIM_MEM)
        s = lax.dot_general(q[:, sl], k[:, sl], _NT, preferred_element_type=F32) * scale
        p = jnp.exp(s - jnp.max(s, axis=1, keepdims=True))
        o = jnp.dot(p.astype(BF16), v[:, sl], preferred_element_type=F32)
        outs.append(o / jnp.sum(p, axis=1, keepdims=True))
    o_ref[...] = jnp.concatenate(outs, axis=1).astype(BF16)


def _mem_attend(q, k, v, tq):
    batch, seq, w = q.shape
    n_mem = k.shape[1]
    kv_spec = pl.BlockSpec((None, n_mem, w), lambda b, i: (b, 0, 0))
    q_spec = pl.BlockSpec((None, tq, w), lambda b, i: (b, i, 0))
    return pl.pallas_call(
        _mem_attend_body,
        out_shape=jax.ShapeDtypeStruct(q.shape, BF16),
        grid=(batch, seq // tq),
        in_specs=[q_spec, kv_spec, kv_spec],
        out_specs=q_spec,
        compiler_params=_cparams(("parallel", "parallel")),
        name="mem_attend",
    )(q, k, v)


def _merge_body(x_ref, oa_ref, ob_ref, om_ref, gt_ref, wa_ref, wb_ref, wm_ref, wo_ref, o_ref):
    merged = None
    for i, (o, w) in enumerate(((oa_ref, wa_ref), (ob_ref, wb_ref), (om_ref, wm_ref))):
        br = jnp.dot(o[...], w[...], preferred_element_type=F32)
        term = gt_ref[:, i * D_MODEL:(i + 1) * D_MODEL].astype(F32) * br
        merged = term if merged is None else merged + term
    o_ref[...] = x_ref[...] + jnp.dot(merged.astype(BF16), wo_ref[...], preferred_element_type=F32)


def _merge(x, oa, ob, om, gates, wa, wb, wm, wo, tm):
    n, d = x.shape
    row = lambda w: pl.BlockSpec((tm, w), lambda i: (i, 0))
    full = lambda a: pl.BlockSpec(a.shape, lambda i: (0, 0))
    return pl.pallas_call(
        _merge_body,
        out_shape=jax.ShapeDtypeStruct((n, d), F32),
        grid=(n // tm,),
        in_specs=[row(d), row(W_ATT), row(W_ATT), row(W_MEM), row(N_BRANCH * d),
                  full(wa), full(wb), full(wm), full(wo)],
        out_specs=row(d),
        compiler_params=_cparams(("parallel",)),
        name="merge",
    )(x, oa, ob, om, gates, wa, wb, wm, wo)


SELECT_PAGES_PER_STEP = 16


def _own_head_columns(acc, n_tok):
    lane_head = lax.broadcasted_iota(jnp.int32, (n_tok, W_ATT), 1) // HEAD_DIM
    out = jnp.zeros((n_tok, W_ATT), F32)
    for h in range(N_HEADS):
        out = out + jnp.where(lane_head == h, acc[h * n_tok:(h + 1) * n_tok, :], 0.0)
    return out


def _suffix_sum_lanes(x):
    lane = lax.broadcasted_iota(jnp.int32, x.shape, 1)
    n = x.shape[1]
    sh = 1
    while sh < n:
        x = x + jnp.where(lane < n - sh, pltpu.roll(x, n - sh, 1), 0.0)
        sh *= 2
    return x


SB_RING = 4


def _sb_decode_body(pt_ref, qbd_ref, knT_ref, vnT_ref, ck_hbm, cv_hbm, o_ref, kbuf, vbuf, sem,
                    *, n_tok, n_pages):
    b = pl.program_id(0)
    qbd = qbd_ref[...] * jnp.asarray(HEAD_DIM ** -0.5, BF16)
    rows = N_HEADS * n_tok
    ahead = SB_RING - 1

    def page_copies(p):
        slot = p % SB_RING
        phys = pt_ref[b, n_pages - 1 - p]
        return (pltpu.make_async_copy(ck_hbm.at[phys], kbuf.at[slot], sem.at[0, slot]),
                pltpu.make_async_copy(cv_hbm.at[phys], vbuf.at[slot], sem.at[1, slot]))

    def start(p):
        for c in page_copies(p):
            c.start()

    def wait(p):
        for c in page_copies(p):
            c.wait()

    def page(kT, vT, lsurv, visible):
        z = jnp.dot(qbd, kT.astype(BF16), preferred_element_type=F32)
        l, ls = _log_sigmoid_pair(z)
        if visible is not None:
            l = jnp.where(visible, l, 0.0)
        suffix = _suffix_sum_lanes(l)
        a = jnp.exp(ls + (suffix - l) + lsurv)
        if visible is not None:
            a = jnp.where(visible, a, 0.0)
        o = lax.dot_general(a.astype(BF16), vT.astype(BF16), _NT, preferred_element_type=F32)
        return o, lsurv + suffix[:, 0:1]

    def alive_flag(lsurv):
        return (jnp.max(lsurv) > LOG_F32_TINY).astype(jnp.int32)

    for p in range(ahead):
        start(p)

    key = lax.broadcasted_iota(jnp.int32, (rows, PAGE_SIZE), 1)
    tok = lax.broadcasted_iota(jnp.int32, (rows, PAGE_SIZE), 0) % n_tok
    acc0, ls0 = page(knT_ref[...], vnT_ref[...], jnp.zeros((rows, 1), F32), key < tok)

    def cond(c):
        return jnp.logical_and(c[0] < n_pages, c[3] > 0)

    def body(c):
        p, lsurv, acc, _ = c
        wait(p)

        @pl.when(p + ahead < n_pages)
        def _():
            start(p + ahead)

        slot = p % SB_RING
        o, lsurv = page(kbuf[slot], vbuf[slot], lsurv, None)
        return p + 1, lsurv, acc + o, alive_flag(lsurv)

    p_end, _, acc, _ = lax.while_loop(cond, body, (0, ls0, acc0, alive_flag(ls0)))

    for j in range(ahead):
        @pl.when(p_end + j < n_pages)
        def _():
            wait(p_end + j)

    o_ref[...] = _own_head_columns(acc, n_tok).astype(BF16)


def _sb_decode(page_table, qbd, knT, vnT, ckT, cvT, n_tok):
    batch, n_pages = page_table.shape
    assert n_pages >= SB_RING
    rows = N_HEADS * n_tok
    per_b = lambda r, c: pl.BlockSpec((None, r, c), lambda b, pt: (b, 0, 0))
    hbm = pl.BlockSpec(memory_space=pl.ANY)
    return pl.pallas_call(
        functools.partial(_sb_decode_body, n_tok=n_tok, n_pages=n_pages),
        out_shape=jax.ShapeDtypeStruct((batch, n_tok, W_ATT), BF16),
        grid_spec=pltpu.PrefetchScalarGridSpec(
            num_scalar_prefetch=1,
            grid=(batch,),
            in_specs=[per_b(rows, W_ATT), per_b(W_ATT, PAGE_SIZE), per_b(W_ATT, PAGE_SIZE), hbm, hbm],
            out_specs=per_b(n_tok, W_ATT),
            scratch_shapes=[pltpu.VMEM((SB_RING, W_ATT, PAGE_SIZE), F32),
                            pltpu.VMEM((SB_RING, W_ATT, PAGE_SIZE), F32),
                            pltpu.SemaphoreType.DMA((2, SB_RING))],
        ),
        compiler_params=_cparams(("arbitrary",)),
        name="sb_decode",
    )(page_table, qbd, knT, vnT, ckT, cvT)


def _moba_select_body(pt_ref, qbd_ref, *refs, n_pages):
    g = SELECT_PAGES_PER_STEP
    k_refs = refs[:g]
    idx_ref, sc_ref = refs[g:]
    step = pl.program_id(1)
    ppb = MOBA_BLOCK // PAGE_SIZE
    qbd = qbd_ref[...]
    rows = qbd.shape[0]
    lane = lax.broadcasted_iota(jnp.int32, (rows, LANES), 1)

    @pl.when(step == 0)
    def _():
        sc_ref[...] = jnp.zeros_like(sc_ref)

    sc = sc_ref[...]
    for i in range(0, g, ppb):
        ksum = k_refs[i][...]
        for j in range(1, ppb):
            ksum = ksum + k_refs[i + j][...]
        z = jnp.dot(qbd, ksum.astype(BF16), preferred_element_type=F32)
        blk = (step * g + i) // ppb
        sc = sc + jnp.where(lane == blk, jnp.sum(z, axis=1, keepdims=True), 0.0)
    sc_ref[...] = sc

    @pl.when(step == n_pages // g - 1)
    def _():
        nb = n_pages // ppb
        s = jnp.where(lane < nb, sc_ref[...], -jnp.inf)
        out = jnp.zeros((rows, LANES), jnp.int32)
        for j in range(MOBA_TOPK):
            best = jnp.max(s, axis=1, keepdims=True)
            pick = jnp.min(jnp.where(s == best, lane, LANES), axis=1, keepdims=True)
            out = jnp.where(lane == j, pick, out)
            s = jnp.where(lane == pick, -jnp.inf, s)
        idx_ref[...] = out


def _moba_select(page_table, qbd, ckT):
    batch, n_pages = page_table.shape
    g = SELECT_PAGES_PER_STEP
    rows = qbd.shape[1]

    def page_spec(i):
        return pl.BlockSpec((None, W_ATT, PAGE_SIZE), lambda b, s, pt: (pt[b, s * g + i], 0, 0))

    return pl.pallas_call(
        functools.partial(_moba_select_body, n_pages=n_pages),
        out_shape=jax.ShapeDtypeStruct((batch, rows, LANES), jnp.int32),
        grid_spec=pltpu.PrefetchScalarGridSpec(
            num_scalar_prefetch=1,
            grid=(batch, n_pages // g),
            in_specs=[pl.BlockSpec((None, rows, W_ATT), lambda b, s, pt: (b, 0, 0))]
            + [page_spec(i) for i in range(g)],
            out_specs=pl.BlockSpec((None, rows, LANES), lambda b, s, pt: (b, 0, 0)),
            scratch_shapes=[pltpu.VMEM((rows, LANES), F32)],
        ),
        compiler_params=_cparams(("parallel", "arbitrary")),
        name="moba_select",
    )(page_table, qbd, *([ckT] * g))


def _moba_gather_body(pt_ref, idx_ref, q_ref, knT_ref, vnT_ref, ck_hbm, cv_hbm, o_ref,
                      kbuf, vbuf, sem, *, n_tok, n_pages):
    ppb = MOBA_BLOCK // PAGE_SIZE
    n_slots = n_tok * MOBA_TOPK
    scale = HEAD_DIM ** -0.5
    step = pl.program_id(0) * N_HEADS + pl.program_id(1)
    n_steps = pl.num_programs(0) * N_HEADS
    half = step % 2

    def start_copies(st, hf):
        bb, hh = st // N_HEADS, st % N_HEADS
        for slot in range(n_slots):
            tok, j = divmod(slot, MOBA_TOPK)
            blk = idx_ref[((bb * N_HEADS + hh) * n_tok + tok) * MOBA_TOPK + j]
            for i in range(ppb):
                pg = pt_ref[bb * n_pages + blk * ppb + i]
                dst = pl.ds(i * PAGE_SIZE, PAGE_SIZE)
                pltpu.make_async_copy(ck_hbm.at[pg, hh], kbuf.at[hf, slot, :, dst], sem.at[0, hf]).start()
                pltpu.make_async_copy(cv_hbm.at[pg, hh], vbuf.at[hf, slot, :, dst], sem.at[1, hf]).start()

    @pl.when(step == 0)
    def _():
        start_copies(step, half)

    @pl.when(step + 1 < n_steps)
    def _():
        start_copies(step + 1, 1 - half)

    kbuf = kbuf.at[half]
    vbuf = vbuf.at[half]
    pltpu.make_async_copy(kbuf, kbuf, sem.at[0, half]).wait()
    pltpu.make_async_copy(vbuf, vbuf, sem.at[1, half]).wait()
    q = q_ref[...]
    tok_row = lax.broadcasted_iota(jnp.int32, (n_tok, MOBA_BLOCK), 0)
    key = lax.broadcasted_iota(jnp.int32, (n_tok, PAGE_SIZE), 1)
    tok = lax.broadcasted_iota(jnp.int32, (n_tok, PAGE_SIZE), 0)
    s_own = jnp.dot(q, knT_ref[...], preferred_element_type=F32) * scale + jnp.where(key <= tok, 0.0, NEG)
    logits = []
    for slot in range(n_slots):
        s = jnp.dot(q, kbuf[slot].astype(BF16), preferred_element_type=F32) * scale
        logits.append(s + jnp.where(tok_row == slot // MOBA_TOPK, 0.0, NEG))
    m = jnp.max(s_own, axis=1, keepdims=True)
    for s in logits:
        m = jnp.maximum(m, jnp.max(s, axis=1, keepdims=True))
    p_own = jnp.exp(s_own - m)
    den = jnp.sum(p_own, axis=1, keepdims=True)
    acc = lax.dot_general(p_own.astype(BF16), vnT_ref[...], _NT, preferred_element_type=F32)
    for slot, s in enumerate(logits):
        p = jnp.exp(s - m)
        den = den + jnp.sum(p, axis=1, keepdims=True)
        acc = acc + lax.dot_general(p.astype(BF16), vbuf[slot].astype(BF16), _NT, preferred_element_type=F32)
    o_ref[...] = (acc / den).astype(BF16)


def _moba_gather(pt_flat, idx_flat, q, knT, vnT, ckT, cvT, n_pages):
    batch, _, n_tok, _ = q.shape
    n_slots = n_tok * MOBA_TOPK
    per_bh = lambda r, c: pl.BlockSpec((None, None, r, c), lambda b, h, pt, ix: (b, h, 0, 0))
    return pl.pallas_call(
        functools.partial(_moba_gather_body, n_tok=n_tok, n_pages=n_pages),
        out_shape=jax.ShapeDtypeStruct((batch, N_HEADS, n_tok, HEAD_DIM), BF16),
        grid_spec=pltpu.PrefetchScalarGridSpec(
            num_scalar_prefetch=2,
            grid=(batch, N_HEADS),
            in_specs=[per_bh(n_tok, HEAD_DIM), per_bh(HEAD_DIM, PAGE_SIZE), per_bh(HEAD_DIM, PAGE_SIZE),
                      pl.BlockSpec(memory_space=pl.ANY), pl.BlockSpec(memory_space=pl.ANY)],
            out_specs=per_bh(n_tok, HEAD_DIM),
            scratch_shapes=[pltpu.VMEM((2, n_slots, HEAD_DIM, MOBA_BLOCK), F32),
                            pltpu.VMEM((2, n_slots, HEAD_DIM, MOBA_BLOCK), F32),
                            pltpu.SemaphoreType.DMA((2, 2))],
        ),
        compiler_params=_cparams(("arbitrary", "arbitrary")),
        name="moba_gather",
    )(pt_flat, idx_flat, q, knT, vnT, ckT, cvT)


def _block_diag_q(q, n_tok):
    batch = q.shape[0] // n_tok
    q3 = q.reshape(batch, 1, n_tok, W_ATT)
    col_head = (jnp.arange(W_ATT) // HEAD_DIM)[None, None, None, :]
    row_head = jnp.arange(N_HEADS)[None, :, None, None]
    return jnp.where(col_head == row_head, q3, 0.0).reshape(batch, N_HEADS * n_tok, W_ATT)


def _new_tokens_T(x, n_tok):
    batch = x.shape[0] // n_tok
    xt = x.reshape(batch, n_tok, W_ATT).transpose(0, 2, 1)
    return jnp.pad(xt, ((0, 0), (0, 0), (0, PAGE_SIZE - n_tok)))


def _cache_T(c):
    return c.transpose(0, 2, 3, 1)


def kernel(x_prompt, x_sample, cache_k_moba, cache_v_moba, cache_k_sb, cache_v_sb, cache_mem_k, cache_mem_v,
           page_table, mem_prompt, norm_ffn1, w_ffn1_gu, w_ffn1_down, norm_mix, w_in, norm_mem, w_mem_kv,
           w_br_moba, w_br_sb, w_br_mem, w_out, norm_ffn2, w_ffn2_gu, w_ffn2_down, norm_final):
    batch, seq, d = x_prompt.shape
    dec_batch, n_tok, _ = x_sample.shape
    n_pages = page_table.shape[1]
    past = n_pages * PAGE_SIZE
    depth = w_in.shape[0]
    assert depth == 1 and past % MOBA_BLOCK == 0 and seq % TOKEN_TILE == 0

    lyr = 0
    bf = lambda w: w[lyr].astype(BF16)
    row = lambda g: g.reshape(1, d)
    w1gu, w1d, w2gu, w2d = bf(w_ffn1_gu), bf(w_ffn1_down), bf(w_ffn2_gu), bf(w_ffn2_down)
    win, wmem, wa, wb, wm, wo = bf(w_in), bf(w_mem_kv), bf(w_br_moba), bf(w_br_sb), bf(w_br_mem), bf(w_out)
    n1, nmix, nmem, n2, nfin = row(norm_ffn1[lyr]), row(norm_mix[lyr]), row(norm_mem[lyr]), row(norm_ffn2[lyr]), row(norm_final)

    n = batch * seq
    nb = seq // MOBA_BLOCK
    xp = _ffn(x_prompt.reshape(n, d), n1, w1gu, w1d, nfin, final_norm=False, tm=1024)
    (qaT, kaT, vaT, ka_nat, vaT_blk, kbar, qbT, kbT, vbT, kb_nat, vbT_blk, qm, gates) = _proj_prompt(
        xp, nmix, win, _rope_tables(jnp.arange(seq)), batch, seq)
    mem_kv = _normproj(mem_prompt.reshape(-1, d), nmem, wmem, tm=512)
    n_mem = mem_prompt.shape[1]
    mk = mem_kv[:, :W_MEM].reshape(batch, n_mem, W_MEM)
    mv = mem_kv[:, W_MEM:].reshape(batch, n_mem, W_MEM)
    oaT = _moba_prompt(qaT, ka_nat.reshape(batch, seq, W_ATT), vaT_blk, kbar.reshape(batch, nb, W_ATT))
    obT = _sb_prompt(qbT, kb_nat.reshape(batch, seq, W_ATT), vbT_blk)
    om = _mem_attend(qm.reshape(batch, seq, W_MEM), mk, mv, tq=512)
    nat = lambda t: t.transpose(0, 2, 1).reshape(n, W_ATT)
    xp = _merge(xp, nat(oaT), nat(obT), om.reshape(n, W_MEM), gates, wa, wb, wm, wo, tm=512)
    y_prompt = _ffn(xp, n2, w2gu, w2d, nfin, final_norm=True, tm=1024).reshape(batch, seq, d)
    rows_out = lambda t: t.reshape(batch, N_HEADS, HEAD_DIM, seq).transpose(0, 3, 1, 2)[None]
    mem_out = lambda t: t.reshape(1, batch, n_mem, N_HEADS_MEM, HEAD_DIM_MEM)

    ns = dec_batch * n_tok
    xs = _ffn(x_sample.reshape(ns, d), n1, w1gu, w1d, nfin, final_norm=False, tm=ns)
    pos_s = jnp.tile(past + jnp.arange(n_tok), dec_batch)
    qa, ka, va, qb, kb, vb, qm_s, gates_s = _proj_sample(xs, nmix, win, _rope_tables(pos_s))

    ck_a, cv_a = _cache_T(cache_k_moba[lyr]), _cache_T(cache_v_moba[lyr])
    ck_b, cv_b = _cache_T(cache_k_sb[lyr]), _cache_T(cache_v_sb[lyr])
    pool = ck_a.shape[0]
    flat = lambda c: c.reshape(pool, W_ATT, PAGE_SIZE)

    idx = _moba_select(page_table, _block_diag_q(qa, n_tok).astype(BF16), flat(ck_a))
    heads = lambda t: t.reshape(dec_batch, n_tok, N_HEADS, HEAD_DIM).transpose(0, 2, 1, 3)
    heads_T = lambda t: _new_tokens_T(t, n_tok).reshape(dec_batch, N_HEADS, HEAD_DIM, PAGE_SIZE)
    oa_s = _moba_gather(page_table.reshape(-1), idx[:, :, :MOBA_TOPK].reshape(-1),
                        heads(qa).astype(BF16), heads_T(ka).astype(BF16), heads_T(va).astype(BF16),
                        ck_a, cv_a, n_pages)
    oa_s = oa_s.transpose(0, 2, 1, 3).reshape(ns, W_ATT)
    ob_s = _sb_decode(page_table, _block_diag_q(qb, n_tok).astype(BF16),
                      _new_tokens_T(kb, n_tok).astype(BF16), _new_tokens_T(vb, n_tok).astype(BF16),
                      flat(ck_b), flat(cv_b), n_tok).reshape(ns, W_ATT)
    om_s = _mem_attend(qm_s.reshape(dec_batch, n_tok, W_MEM),
                       cache_mem_k[lyr].reshape(dec_batch, -1, W_MEM),
                       cache_mem_v[lyr].reshape(dec_batch, -1, W_MEM), tq=n_tok).reshape(ns, W_MEM)
    xs = _merge(xs, oa_s, ob_s, om_s, gates_s, wa, wb, wm, wo, tm=ns)
    y_sample = _ffn(xs, n2, w2gu, w2d, nfin, final_norm=True, tm=ns).reshape(dec_batch, n_tok, d)
    new_rows = lambda t: t.reshape(1, dec_batch, n_tok, N_HEADS, HEAD_DIM)

    return (y_prompt, y_sample, rows_out(kaT), rows_out(vaT), rows_out(kbT), rows_out(vbT),
            mem_out(mk), mem_out(mv), new_rows(ka), new_rows(va), new_rows(kb), new_rows(vb))
```

```python
import functools

import jax
import jax.numpy as jnp
from jax import lax
from jax.experimental import pallas as pl
from jax.experimental.pallas import tpu as pltpu

F32 = jnp.float32
BF16 = jnp.bfloat16

D_MODEL = 1024
HEAD_DIM = 64
N_HEADS = 8
W_ATT = N_HEADS * HEAD_DIM
N_HEADS_MEM = 4
HEAD_DIM_MEM = 128
W_MEM = N_HEADS_MEM * HEAD_DIM_MEM
N_BRANCH = 3
D_IN = 6 * W_ATT + W_MEM + N_BRANCH * D_MODEL
D_FF = 2816
MOBA_BLOCK = 256
MOBA_TOPK = 3
PAGE_SIZE = 128
ROT_DIM = HEAD_DIM // 4
ROPE_THETA = 500000.0
EPS = 1e-6

NEG = -1e30
LOG_F32_TINY = -88.0
LOG2_E = 1.4426950408889634
FF_CHUNK = 256
TOKEN_TILE = 256
LANES = 128
VMEM_LIMIT_BYTES = 56 * 1024 * 1024

_NT = (((1,), (1,)), ((), ()))


def _cparams(sem):
    return pltpu.CompilerParams(dimension_semantics=sem, vmem_limit_bytes=VMEM_LIMIT_BYTES)


def _rms(x, g):
    return x * lax.rsqrt(jnp.mean(x * x, axis=-1, keepdims=True) + EPS) * g


def _softplus_pair(z):
    t = jnp.log(1.0 + jnp.exp2(jnp.abs(z) * -LOG2_E))
    return jnp.maximum(z, 0.0) + t, jnp.minimum(z, 0.0) - t


def _log_sigmoid_pair(z):
    sp, lsig = _softplus_pair(z)
    return -sp, lsig


def _ffn_body(x_ref, g_ref, wgu_ref, wd_ref, gf_ref, o_ref, act_ref, *, final_norm):
    h = _rms(x_ref[...], g_ref[...]).astype(BF16)
    for c in range(D_FF // FF_CHUNK):
        cols = slice(c * FF_CHUNK, (c + 1) * FF_CHUNK)
        up_cols = slice(D_FF + c * FF_CHUNK, D_FF + (c + 1) * FF_CHUNK)
        gate = jnp.dot(h, wgu_ref[:, cols], preferred_element_type=F32)
        up = jnp.dot(h, wgu_ref[:, up_cols], preferred_element_type=F32)
        act_ref[:, cols] = (gate * jax.nn.sigmoid(gate) * up).astype(BF16)
    y = x_ref[...] + 0.5 * jnp.dot(act_ref[...], wd_ref[...], preferred_element_type=F32)
    if final_norm:
        y = _rms(y, gf_ref[...])
    o_ref[...] = y


def _ffn(x, g, w_gu, w_down, g_final, *, final_norm, tm):
    n, d = x.shape
    resident = lambda a: pl.BlockSpec(a.shape, lambda i: (0, 0), pipeline_mode=pl.Buffered(1))
    return pl.pallas_call(
        functools.partial(_ffn_body, final_norm=final_norm),
        out_shape=jax.ShapeDtypeStruct((n, d), F32),
        grid=(n // tm,),
        in_specs=[
            pl.BlockSpec((tm, d), lambda i: (i, 0)),
            pl.BlockSpec((1, d), lambda i: (0, 0)),
            resident(w_gu),
            resident(w_down),
            pl.BlockSpec((1, d), lambda i: (0, 0)),
        ],
        out_specs=pl.BlockSpec((tm, d), lambda i: (i, 0)),
        scratch_shapes=[pltpu.VMEM((tm, D_FF), BF16)],
        compiler_params=_cparams(("parallel",)),
        name="ffn",
    )(x, g, w_gu, w_down, g_final)


def _rope_tables(pos):
    half = ROT_DIM // 2
    inv = ROPE_THETA ** (-(jnp.arange(half, dtype=F32) * 2.0) / ROT_DIM)
    ang = pos.astype(F32)[:, None] * inv[None, :]
    cos, sin = jnp.cos(ang), jnp.sin(ang)
    t = pos.shape[0]
    rest = HEAD_DIM - ROT_DIM
    z8 = jnp.zeros((t, half), F32)
    cos_h = jnp.concatenate([cos, cos, jnp.ones((t, rest), F32)], axis=1)
    sa_h = jnp.concatenate([-sin, z8, jnp.zeros((t, rest), F32)], axis=1)
    sb_h = jnp.concatenate([z8, sin, jnp.zeros((t, rest), F32)], axis=1)
    two = lambda a: jnp.concatenate([a, a], axis=1)
    return two(cos_h), two(sa_h), two(sb_h)


def _rope(z, cos, sa, sb):
    half = ROT_DIM // 2
    w = z.shape[1]
    return z * cos + pltpu.roll(z, w - half, 1) * sa + pltpu.roll(z, half, 1) * sb


def _proj_common(x_ref, g_ref, w_ref, cos_ref, sa_ref, sb_ref):
    h = _rms(x_ref[...], g_ref[...]).astype(BF16)
    reps = W_ATT // LANES
    wide = lambda r: jnp.concatenate([r[...]] * reps, axis=1)
    cos, sa, sb = wide(cos_ref), wide(sa_ref), wide(sb_ref)

    def col(c):
        return jnp.dot(h, w_ref[:, c * W_ATT:(c + 1) * W_ATT], preferred_element_type=F32)

    rope = lambda z: _rope(z, cos, sa, sb)
    return col, rope


def _store_gates(col, gt_ref):
    for c in range(N_BRANCH * D_MODEL // W_ATT):
        gt_ref[:, c * W_ATT:(c + 1) * W_ATT] = jax.nn.sigmoid(col(7 + c)).astype(BF16)


def _proj_prompt_body(x_ref, g_ref, w_ref, cos_ref, sa_ref, sb_ref,
                      qaT_ref, kaT_ref, vaT_ref, kan_ref, vaTb_ref, kbar_ref,
                      qbT_ref, kbT_ref, vbT_ref, kbn_ref, vbTb_ref, qm_ref, gt_ref):
    col, rope = _proj_common(x_ref, g_ref, w_ref, cos_ref, sa_ref, sb_ref)
    qaT_ref[...] = rope(col(0)).T.astype(BF16)
    ka = rope(col(1))
    kaT_ref[...] = ka.T
    kan_ref[...] = ka.astype(BF16)
    kbar_ref[...] = jnp.sum(ka, axis=0, keepdims=True) * (1.0 / MOBA_BLOCK)
    vaT = col(2).T
    vaT_ref[...] = vaT
    vaTb_ref[...] = vaT.astype(BF16)
    qbT_ref[...] = col(3).T.astype(BF16)
    kb = col(4)
    kbT_ref[...] = kb.T
    kbn_ref[...] = kb.astype(BF16)
    vbT = col(5).T
    vbT_ref[...] = vbT
    vbTb_ref[...] = vbT.astype(BF16)
    qm_ref[...] = col(6).astype(BF16)
    _store_gates(col, gt_ref)


def _proj_sample_body(x_ref, g_ref, w_ref, cos_ref, sa_ref, sb_ref,
                      qa_ref, ka_ref, va_ref, qb_ref, kb_ref, vb_ref, qm_ref, gt_ref):
    col, rope = _proj_common(x_ref, g_ref, w_ref, cos_ref, sa_ref, sb_ref)
    qa_ref[...] = rope(col(0))
    ka_ref[...] = rope(col(1))
    va_ref[...] = col(2)
    qb_ref[...] = col(3)
    kb_ref[...] = col(4)
    vb_ref[...] = col(5)
    qm_ref[...] = col(6).astype(BF16)
    _store_gates(col, gt_ref)


def _proj_in_specs(tm, n_tab_tiles):
    tab = pl.BlockSpec((tm, LANES), lambda i: (i % n_tab_tiles, 0))
    return [
        pl.BlockSpec((tm, D_MODEL), lambda i: (i, 0)),
        pl.BlockSpec((1, D_MODEL), lambda i: (0, 0)),
        pl.BlockSpec((D_MODEL, D_IN), lambda i: (0, 0), pipeline_mode=pl.Buffered(1)),
        tab, tab, tab,
    ]


def _proj_prompt(x, g, w_in, tables, batch, seq):
    tm = TOKEN_TILE
    nb = seq // tm
    n = batch * seq
    sds = jax.ShapeDtypeStruct
    t_spec = pl.BlockSpec((None, W_ATT, tm), lambda i: (i // nb, 0, i % nb))
    blk_spec = pl.BlockSpec((None, None, W_ATT, tm), lambda i: (i // nb, i % nb, 0, 0))
    nat_spec = pl.BlockSpec((tm, W_ATT), lambda i: (i, 0))
    t_bf = sds((batch, W_ATT, seq), BF16)
    t_f32 = sds((batch, W_ATT, seq), F32)
    blk_bf = sds((batch, nb, W_ATT, tm), BF16)
    nat_bf = sds((n, W_ATT), BF16)
    out_shape = [t_bf, t_f32, t_f32, nat_bf, blk_bf, sds((n // tm, 1, W_ATT), F32),
                 t_bf, t_f32, t_f32, nat_bf, blk_bf, nat_bf, sds((n, N_BRANCH * D_MODEL), BF16)]
    out_specs = [t_spec, t_spec, t_spec, nat_spec, blk_spec,
                 pl.BlockSpec((None, 1, W_ATT), lambda i: (i, 0, 0)),
                 t_spec, t_spec, t_spec, nat_spec, blk_spec, nat_spec,
                 pl.BlockSpec((tm, N_BRANCH * D_MODEL), lambda i: (i, 0))]
    return pl.pallas_call(
        _proj_prompt_body,
        out_shape=out_shape,
        grid=(n // tm,),
        in_specs=_proj_in_specs(tm, nb),
        out_specs=out_specs,
        compiler_params=_cparams(("parallel",)),
        name="proj_prompt",
    )(x, g, w_in, *tables)


def _proj_sample(x, g, w_in, tables):
    n = x.shape[0]
    sds = jax.ShapeDtypeStruct
    nat = lambda dt: sds((n, W_ATT), dt)
    spec = pl.BlockSpec((n, W_ATT), lambda i: (0, 0))
    out_shape = [nat(F32)] * 6 + [nat(BF16), sds((n, N_BRANCH * D_MODEL), BF16)]
    out_specs = [spec] * 7 + [pl.BlockSpec((n, N_BRANCH * D_MODEL), lambda i: (0, 0))]
    return pl.pallas_call(
        _proj_sample_body,
        out_shape=out_shape,
        grid=(1,),
        in_specs=_proj_in_specs(n, 1),
        out_specs=out_specs,
        compiler_params=_cparams(("arbitrary",)),
        name="proj_sample",
    )(x, g, w_in, *tables)


def _normproj_body(x_ref, g_ref, w_ref, o_ref):
    h = _rms(x_ref[...], g_ref[...]).astype(BF16)
    o_ref[...] = jnp.dot(h, w_ref[...], preferred_element_type=F32)


def _normproj(x, g, w, tm):
    n, d = x.shape
    dout = w.shape[1]
    return pl.pallas_call(
        _normproj_body,
        out_shape=jax.ShapeDtypeStruct((n, dout), F32),
        grid=(n // tm,),
        in_specs=[pl.BlockSpec((tm, d), lambda i: (i, 0)),
                  pl.BlockSpec((1, d), lambda i: (0, 0)),
                  pl.BlockSpec((d, dout), lambda i: (0, 0))],
        out_specs=pl.BlockSpec((tm, dout), lambda i: (i, 0)),
        compiler_params=_cparams(("parallel",)),
        name="mem_kv",
    )(x, g, w)


def _head_masked(qT):
    row = lax.broadcasted_iota(jnp.int32, qT.shape, 0)
    zero = jnp.zeros_like(qT)
    return [jnp.where(row < HEAD_DIM, qT, zero), jnp.where(row >= HEAD_DIM, qT, zero)]


MOBA_SUM_ROWS = 16
MOBA_PAIRS_PER_STEP = 2


def _moba_prompt_body(qT_ref, k_ref, vT_ref, kbar_ref, oT_ref, sel_ref, m_ref, pv_ref,
                      sa_ref, sb_ref, pa_ref, pb_ref, *, nb):
    qi = pl.program_id(2)
    tq = qT_ref.shape[1]
    pw = 2 * HEAD_DIM
    chains = range(MOBA_PAIRS_PER_STEP)
    pair_rows = lambda c: slice(c * pw, (c + 1) * pw)
    nidx = lax.broadcasted_iota(jnp.int32, (nb, 2 * tq), 0)
    qs = []
    for c in chains:
        q2 = jnp.concatenate(_head_masked(qT_ref[pair_rows(c), :]), axis=1)
        qs.append(q2 * jnp.asarray(HEAD_DIM ** -0.5, BF16))
        sc = jnp.dot(kbar_ref[:, pair_rows(c)].astype(BF16), q2, preferred_element_type=F32)
        sc = jnp.where(nidx < qi, sc, -jnp.inf)
        rank = jnp.zeros((nb, 2 * tq), F32)
        for m in range(nb):
            r = sc[m:m + 1, :]
            beats = (r > sc) | ((r == sc) & (m < nidx))
            rank = rank + jnp.where(beats, 1.0, 0.0)
        sel_ref[c] = jnp.where((nidx < qi) & (rank < MOBA_TOPK), 0.0, NEG)

    def key_block(c, n):
        return k_ref[pl.ds(pl.multiple_of(n * MOBA_BLOCK, MOBA_BLOCK), MOBA_BLOCK), pair_rows(c)]

    def past_logits(c, e):
        n = e - 1
        return jnp.dot(key_block(c, n), qs[c], preferred_element_type=F32) + sel_ref[c, pl.ds(n, 1), :]

    def weights_into(c, e, s_ref, p_ref):
        s = s_ref[c]
        m = jnp.max(s, axis=0, keepdims=True)
        m_ref[c, pl.ds(e, 1), :] = m
        p_ref[c] = jnp.exp(s - m).astype(BF16)

    ones_rows = jnp.ones((MOBA_SUM_ROWS, MOBA_BLOCK), BF16)

    def pv_into(c, slot, p_ref, v_idx):
        pb = p_ref[c]
        vn = vT_ref[v_idx, pair_rows(c), :]
        lhs = [jnp.concatenate([vn[hh * HEAD_DIM:(hh + 1) * HEAD_DIM, :], ones_rows], axis=0) for hh in range(2)]
        pv_ref[c, slot] = jnp.concatenate(
            [jnp.dot(lhs[0], pb[:, :tq], preferred_element_type=F32),
             jnp.dot(lhs[1], pb[:, tq:], preferred_element_type=F32)], axis=1)

    def value_block(e):
        return jnp.where(e == 0, qi, e - 1)

    kpos = lax.broadcasted_iota(jnp.int32, (MOBA_BLOCK, 2 * tq), 0)
    qpos = lax.broadcasted_iota(jnp.int32, (MOBA_BLOCK, 2 * tq), 1) % tq
    causal = jnp.where(kpos <= qpos, 0.0, NEG)
    for c in chains:
        sa_ref[c] = jnp.dot(key_block(c, qi), qs[c], preferred_element_type=F32) + causal
    pb_ref[...] = jnp.zeros_like(pb_ref)
    m_ref[...] = jnp.full(m_ref.shape, NEG, F32)
    spare = nb

    def pair(g, carry):
        e = 2 * g
        for c in chains:
            pv_into(c, jnp.where(e == 0, spare, e - 1), pb_ref, jnp.where(e == 0, 0, value_block(e - 1)))
        for c in chains:
            sb_ref[c] = past_logits(c, e + 1)
        for c in chains:
            weights_into(c, e, sa_ref, pa_ref)
        for c in chains:
            pv_into(c, e, pa_ref, value_block(e))
        for c in chains:
            sa_ref[c] = past_logits(c, e + 2)
        for c in chains:
            weights_into(c, e + 1, sb_ref, pb_ref)
        return carry

    n_pairs = qi // 2 + 1
    lax.fori_loop(0, n_pairs, pair, 0)
    n_elem = 2 * n_pairs
    for c in chains:
        pv_into(c, n_elem - 1, pb_ref, n_elem - 2)

    m_all = [jnp.max(m_ref[c], axis=0, keepdims=True) for c in chains]

    def merge(e, accs):
        return tuple(acc + jnp.exp(m_ref[c, pl.ds(e, 1), :] - m_all[c]) * pv_ref[c, e]
                     for c, acc in zip(chains, accs))

    zero = jnp.zeros((HEAD_DIM + MOBA_SUM_ROWS, 2 * tq), F32)
    accs = lax.fori_loop(0, n_elem, merge, tuple(zero for _ in chains))
    for c, acc in zip(chains, accs):
        o = acc[:HEAD_DIM, :] / acc[HEAD_DIM:HEAD_DIM + 1, :]
        oT_ref[pair_rows(c), :] = jnp.concatenate([o[:, :tq], o[:, tq:]], axis=0).astype(BF16)


def _sb_prompt_body(qT_ref, k_ref, vT_ref, oT_ref):
    qi = pl.program_id(2)
    tq = qT_ref.shape[1]
    qs = jnp.concatenate(_head_masked(qT_ref[...]), axis=1) * jnp.asarray(HEAD_DIM ** -0.5, BF16)
    kk = lax.broadcasted_iota(jnp.int32, (MOBA_BLOCK, MOBA_BLOCK), 0)
    jj = lax.broadcasted_iota(jnp.int32, (MOBA_BLOCK, MOBA_BLOCK), 1)
    neg_later = jnp.where(kk < jj, -1.0, 0.0).astype(BF16)
    kpos = lax.broadcasted_iota(jnp.int32, (MOBA_BLOCK, 2 * tq), 0)
    qpos = lax.broadcasted_iota(jnp.int32, (MOBA_BLOCK, 2 * tq), 1) % tq

    def blocks_terms(blocks):
        zs = [jnp.dot(k_ref[pl.ds(pl.multiple_of(n * MOBA_BLOCK, MOBA_BLOCK), MOBA_BLOCK), :], qs,
                      preferred_element_type=F32) for n, _ in blocks]
        pairs = [_softplus_pair(z) for z in zs]
        sps = [sp if vis is None else jnp.where(vis, sp, 0.0) for (sp, _), (_, vis) in zip(pairs, blocks)]
        rs = [jnp.dot(neg_later, sp.astype(BF16), preferred_element_type=F32) for sp in sps]
        out = []
        for (n, vis), (_, lsig), sp, r in zip(blocks, pairs, sps, rs):
            a = jnp.exp(lsig + r)
            if vis is not None:
                a = jnp.where(vis, a, 0.0)
            ab = a.astype(BF16)
            vn = vT_ref[n]
            pv = jnp.concatenate(
                [jnp.dot(vn[:HEAD_DIM, :], ab[:, :tq], preferred_element_type=F32),
                 jnp.dot(vn[HEAD_DIM:, :], ab[:, tq:], preferred_element_type=F32)], axis=1)
            out.append((pv, r[0:1, :] - sp[0:1, :]))
        return out

    (pv_d, tot_d), (pv_p, tot_p) = blocks_terms([(qi, kpos < qpos), (jnp.maximum(qi - 1, 0), None)])
    has_prev = qi > 0
    acc = pv_d + jnp.where(has_prev, jnp.exp(tot_d), 0.0) * pv_p
    lsurv = tot_d + jnp.where(has_prev, tot_p, 0.0)

    def alive_flag(ls):
        return (jnp.max(ls) > LOG_F32_TINY).astype(jnp.int32)

    def cond(c):
        return jnp.logical_and(c[0] >= 0, c[3] > 0)

    def body(c):
        n, ls, acc, _ = c
        (pv, tot), = blocks_terms([(n, None)])
        ls_new = ls + tot
        return n - 1, ls_new, acc + jnp.exp(ls) * pv, alive_flag(ls_new)

    fin = lax.while_loop(cond, body, (qi - 2, lsurv, acc, alive_flag(lsurv)))
    o = fin[2]
    oT_ref[...] = jnp.concatenate([o[:, :tq], o[:, tq:]], axis=0).astype(BF16)


def _prompt_mixer_specs(seq, nb, pairs):
    tq = TOKEN_TILE
    w = pairs * 2 * HEAD_DIM
    q_spec = pl.BlockSpec((None, w, tq), lambda b, hp, qi: (b, hp, qi))
    k_spec = pl.BlockSpec((None, seq, w), lambda b, hp, qi: (b, 0, hp))
    v_spec = pl.BlockSpec((None, nb, w, MOBA_BLOCK), lambda b, hp, qi: (b, 0, hp, 0))
    return q_spec, k_spec, v_spec


def _moba_prompt(qT, k_nat, vT_blk, kbar):
    batch, _, seq = qT.shape
    nb = seq // MOBA_BLOCK
    pairs = MOBA_PAIRS_PER_STEP
    q_spec, k_spec, v_spec = _prompt_mixer_specs(seq, nb, pairs)
    cols = 2 * TOKEN_TILE
    return pl.pallas_call(
        functools.partial(_moba_prompt_body, nb=nb),
        out_shape=jax.ShapeDtypeStruct(qT.shape, BF16),
        grid=(batch, N_HEADS // (2 * pairs), seq // TOKEN_TILE),
        in_specs=[q_spec, k_spec, v_spec,
                  pl.BlockSpec((None, nb, pairs * 2 * HEAD_DIM), lambda b, hp, qi: (b, 0, hp))],
        out_specs=q_spec,
        scratch_shapes=[pltpu.VMEM((pairs, nb, cols), F32)] * 2
        + [pltpu.VMEM((pairs, nb + 1, HEAD_DIM + MOBA_SUM_ROWS, cols), F32)]
        + [pltpu.VMEM((pairs, MOBA_BLOCK, cols), F32)] * 2
        + [pltpu.VMEM((pairs, MOBA_BLOCK, cols), BF16)] * 2,
        compiler_params=_cparams(("parallel", "parallel", "arbitrary")),
        name="moba_prompt",
    )(qT, k_nat, vT_blk, kbar)


def _sb_prompt(qT, k_nat, vT_blk):
    batch, _, seq = qT.shape
    nb = seq // MOBA_BLOCK
    q_spec, k_spec, v_spec = _prompt_mixer_specs(seq, nb, 1)
    return pl.pallas_call(
        _sb_prompt_body,
        out_shape=jax.ShapeDtypeStruct(qT.shape, BF16),
        grid=(batch, N_HEADS // 2, seq // TOKEN_TILE),
        in_specs=[q_spec, k_spec, v_spec],
        out_specs=q_spec,
        compiler_params=_cparams(("parallel", "parallel", "arbitrary")),
        name="sb_prompt",
    )(qT, k_nat, vT_blk)


def _mem_attend_body(q_ref, k_ref, v_ref, o_ref):
    q = q_ref[...]
    k = k_ref[...].astype(BF16)
    v = v_ref[...].astype(BF16)
    scale = HEAD_DIM_MEM ** -0.5
    outs = []
    for h in range(N_HEADS_MEM):
        sl = slice(h * HEAD_DIM_MEM, (h + 1) * HEAD_DIM_MEM)
        s = lax.dot_general(q[:, sl], k[:, sl], _NT, preferred_element_type=F32) * scale
        p = jnp.exp(s - jnp.max(s, axis=1, keepdims=True))
        o = jnp.dot(p.astype(BF16), v[:, sl], preferred_element_type=F32)
        outs.append(o / jnp.sum(p, axis=1, keepdims=True))
    o_ref[...] = jnp.concatenate(outs, axis=1).astype(BF16)


def _mem_attend(q, k, v, tq):
    batch, seq, w = q.shape
    n_mem = k.shape[1]
    kv_spec = pl.BlockSpec((None, n_mem, w), lambda b, i: (b, 0, 0))
    q_spec = pl.BlockSpec((None, tq, w), lambda b, i: (b, i, 0))
    return pl.pallas_call(
        _mem_attend_body,
        out_shape=jax.ShapeDtypeStruct(q.shape, BF16),
        grid=(batch, seq // tq),
        in_specs=[q_spec, kv_spec, kv_spec],
        out_specs=q_spec,
        compiler_params=_cparams(("parallel", "parallel")),
        name="mem_attend",
    )(q, k, v)


def _merge_body(x_ref, oa_ref, ob_ref, om_ref, gt_ref, wa_ref, wb_ref, wm_ref, wo_ref, o_ref):
    merged = None
    for i, (o, w) in enumerate(((oa_ref, wa_ref), (ob_ref, wb_ref), (om_ref, wm_ref))):
        br = jnp.dot(o[...], w[...], preferred_element_type=F32)
        term = gt_ref[:, i * D_MODEL:(i + 1) * D_MODEL].astype(F32) * br
        merged = term if merged is None else merged + term
    o_ref[...] = x_ref[...] + jnp.dot(merged.astype(BF16), wo_ref[...], preferred_element_type=F32)


def _merge(x, oa, ob, om, gates, wa, wb, wm, wo, tm):
    n, d = x.shape
    row = lambda w: pl.BlockSpec((tm, w), lambda i: (i, 0))
    full = lambda a: pl.BlockSpec(a.shape, lambda i: (0, 0))
    return pl.pallas_call(
        _merge_body,
        out_shape=jax.ShapeDtypeStruct((n, d), F32),
        grid=(n // tm,),
        in_specs=[row(d), row(W_ATT), row(W_ATT), row(W_MEM), row(N_BRANCH * d),
                  full(wa), full(wb), full(wm), full(wo)],
        out_specs=row(d),
        compiler_params=_cparams(("parallel",)),
        name="merge",
    )(x, oa, ob, om, gates, wa, wb, wm, wo)


SELECT_PAGES_PER_STEP = 16


def _own_head_columns(acc, n_tok):
    lane_head = lax.broadcasted_iota(jnp.int32, (n_tok, W_ATT), 1) // HEAD_DIM
    out = jnp.zeros((n_tok, W_ATT), F32)
    for h in range(N_HEADS):
        out = out + jnp.where(lane_head == h, acc[h * n_tok:(h + 1) * n_tok, :], 0.0)
    return out


def _suffix_sum_lanes(x):
    lane = lax.broadcasted_iota(jnp.int32, x.shape, 1)
    n = x.shape[1]
    sh = 1
    while sh < n:
        x = x + jnp.where(lane < n - sh, pltpu.roll(x, n - sh, 1), 0.0)
        sh *= 2
    return x


SB_RING = 4


def _sb_decode_body(pt_ref, qbd_ref, knT_ref, vnT_ref, ck_hbm, cv_hbm, o_ref, kbuf, vbuf, sem,
                    *, n_tok, n_pages):
    b = pl.program_id(0)
    qbd = qbd_ref[...] * jnp.asarray(HEAD_DIM ** -0.5, BF16)
    rows = N_HEADS * n_tok
    ahead = SB_RING - 1

    def page_copies(p):
        slot = p % SB_RING
        phys = pt_ref[b, n_pages - 1 - p]
        return (pltpu.make_async_copy(ck_hbm.at[phys], kbuf.at[slot], sem.at[0, slot]),
                pltpu.make_async_copy(cv_hbm.at[phys], vbuf.at[slot], sem.at[1, slot]))

    def start(p):
        for c in page_copies(p):
            c.start()

    def wait(p):
        for c in page_copies(p):
            c.wait()

    def page(kT, vT, lsurv, visible):
        z = jnp.dot(qbd, kT.astype(BF16), preferred_element_type=F32)
        l, ls = _log_sigmoid_pair(z)
        if visible is not None:
            l = jnp.where(visible, l, 0.0)
        suffix = _suffix_sum_lanes(l)
        a = jnp.exp(ls + (suffix - l) + lsurv)
        if visible is not None:
            a = jnp.where(visible, a, 0.0)
        o = lax.dot_general(a.astype(BF16), vT.astype(BF16), _NT, preferred_element_type=F32)
        return o, lsurv + suffix[:, 0:1]

    def alive_flag(lsurv):
        return (jnp.max(lsurv) > LOG_F32_TINY).astype(jnp.int32)

    for p in range(ahead):
        start(p)

    key = lax.broadcasted_iota(jnp.int32, (rows, PAGE_SIZE), 1)
    tok = lax.broadcasted_iota(jnp.int32, (rows, PAGE_SIZE), 0) % n_tok
    acc0, ls0 = page(knT_ref[...], vnT_ref[...], jnp.zeros((rows, 1), F32), key < tok)

    def cond(c):
        return jnp.logical_and(c[0] < n_pages, c[3] > 0)

    def body(c):
        p, lsurv, acc, _ = c
        wait(p)

        @pl.when(p + ahead < n_pages)
        def _():
            start(p + ahead)

        slot = p % SB_RING
        o, lsurv = page(kbuf[slot], vbuf[slot], lsurv, None)
        return p + 1, lsurv, acc + o, alive_flag(lsurv)

    p_end, _, acc, _ = lax.while_loop(cond, body, (0, ls0, acc0, alive_flag(ls0)))

    for j in range(ahead):
        @pl.when(p_end + j < n_pages)
        def _():
            wait(p_end + j)

    o_ref[...] = _own_head_columns(acc, n_tok).astype(BF16)


def _sb_decode(page_table, qbd, knT, vnT, ckT, cvT, n_tok):
    batch, n_pages = page_table.shape
    assert n_pages >= SB_RING
    rows = N_HEADS * n_tok
    per_b = lambda r, c: pl.BlockSpec((None, r, c), lambda b, pt: (b, 0, 0))
    hbm = pl.BlockSpec(memory_space=pl.ANY)
    return pl.pallas_call(
        functools.partial(_sb_decode_body, n_tok=n_tok, n_pages=n_pages),
        out_shape=jax.ShapeDtypeStruct((batch, n_tok, W_ATT), BF16),
        grid_spec=pltpu.PrefetchScalarGridSpec(
            num_scalar_prefetch=1,
            grid=(batch,),
            in_specs=[per_b(rows, W_ATT), per_b(W_ATT, PAGE_SIZE), per_b(W_ATT, PAGE_SIZE), hbm, hbm],
            out_specs=per_b(n_tok, W_ATT),
            scratch_shapes=[pltpu.VMEM((SB_RING, W_ATT, PAGE_SIZE), F32),
                            pltpu.VMEM((SB_RING, W_ATT, PAGE_SIZE), F32),
                            pltpu.SemaphoreType.DMA((2, SB_RING))],
        ),
        compiler_params=_cparams(("arbitrary",)),
        name="sb_decode",
    )(page_table, qbd, knT, vnT, ckT, cvT)


def _moba_select_body(pt_ref, qbd_ref, *refs, n_pages):
    g = SELECT_PAGES_PER_STEP
    k_refs = refs[:g]
    idx_ref, sc_ref = refs[g:]
    step = pl.program_id(1)
    ppb = MOBA_BLOCK // PAGE_SIZE
    qbd = qbd_ref[...]
    rows = qbd.shape[0]
    lane = lax.broadcasted_iota(jnp.int32, (rows, LANES), 1)

    @pl.when(step == 0)
    def _():
        sc_ref[...] = jnp.zeros_like(sc_ref)

    sc = sc_ref[...]
    for i in range(0, g, ppb):
        ksum = k_refs[i][...]
        for j in range(1, ppb):
            ksum = ksum + k_refs[i + j][...]
        z = jnp.dot(qbd, ksum.astype(BF16), preferred_element_type=F32)
        blk = (step * g + i) // ppb
        sc = sc + jnp.where(lane == blk, jnp.sum(z, axis=1, keepdims=True), 0.0)
    sc_ref[...] = sc

    @pl.when(step == n_pages // g - 1)
    def _():
        nb = n_pages // ppb
        s = jnp.where(lane < nb, sc_ref[...], -jnp.inf)
        out = jnp.zeros((rows, LANES), jnp.int32)
        for j in range(MOBA_TOPK):
            best = jnp.max(s, axis=1, keepdims=True)
            pick = jnp.min(jnp.where(s == best, lane, LANES), axis=1, keepdims=True)
            out = jnp.where(lane == j, pick, out)
            s = jnp.where(lane == pick, -jnp.inf, s)
        idx_ref[...] = out


def _moba_select(page_table, qbd, ckT):
    batch, n_pages = page_table.shape
    g = SELECT_PAGES_PER_STEP
    rows = qbd.shape[1]

    def page_spec(i):
        return pl.BlockSpec((None, W_ATT, PAGE_SIZE), lambda b, s, pt: (pt[b, s * g + i], 0, 0))

    return pl.pallas_call(
        functools.partial(_moba_select_body, n_pages=n_pages),
        out_shape=jax.ShapeDtypeStruct((batch, rows, LANES), jnp.int32),
        grid_spec=pltpu.PrefetchScalarGridSpec(
            num_scalar_prefetch=1,
            grid=(batch, n_pages // g),
            in_specs=[pl.BlockSpec((None, rows, W_ATT), lambda b, s, pt: (b, 0, 0))]
            + [page_spec(i) for i in range(g)],
            out_specs=pl.BlockSpec((None, rows, LANES), lambda b, s, pt: (b, 0, 0)),
            scratch_shapes=[pltpu.VMEM((rows, LANES), F32)],
        ),
        compiler_params=_cparams(("parallel", "arbitrary")),
        name="moba_select",
    )(page_table, qbd, *([ckT] * g))


def _moba_gather_body(pt_ref, idx_ref, q_ref, knT_ref, vnT_ref, ck_hbm, cv_hbm, o_ref,
                      kbuf, vbuf, sem, *, n_tok, n_pages):
    ppb = MOBA_BLOCK // PAGE_SIZE
    n_slots = n_tok * MOBA_TOPK
    scale = HEAD_DIM ** -0.5
    step = pl.program_id(0) * N_HEADS + pl.program_id(1)
    n_steps = pl.num_programs(0) * N_HEADS
    half = step % 2

    def start_copies(st, hf):
        bb, hh = st // N_HEADS, st % N_HEADS
        for slot in range(n_slots):
            tok, j = divmod(slot, MOBA_TOPK)
            blk = idx_ref[((bb * N_HEADS + hh) * n_tok + tok) * MOBA_TOPK + j]
            for i in range(ppb):
                pg = pt_ref[bb * n_pages + blk * ppb + i]
                dst = pl.ds(i * PAGE_SIZE, PAGE_SIZE)
                pltpu.make_async_copy(ck_hbm.at[pg, hh], kbuf.at[hf, slot, :, dst], sem.at[0, hf]).start()
                pltpu.make_async_copy(cv_hbm.at[pg, hh], vbuf.at[hf, slot, :, dst], sem.at[1, hf]).start()

    @pl.when(step == 0)
    def _():
        start_copies(step, half)

    @pl.when(step + 1 < n_steps)
    def _():
        start_copies(step + 1, 1 - half)

    kbuf = kbuf.at[half]
    vbuf = vbuf.at[half]
    pltpu.make_async_copy(kbuf, kbuf, sem.at[0, half]).wait()
    pltpu.make_async_copy(vbuf, vbuf, sem.at[1, half]).wait()
    q = q_ref[...]
    tok_row = lax.broadcasted_iota(jnp.int32, (n_tok, MOBA_BLOCK), 0)
    key = lax.broadcasted_iota(jnp.int32, (n_tok, PAGE_SIZE), 1)
    tok = lax.broadcasted_iota(jnp.int32, (n_tok, PAGE_SIZE), 0)
    s_own = jnp.dot(q, knT_ref[...], preferred_element_type=F32) * scale + jnp.where(key <= tok, 0.0, NEG)
    logits = []
    for slot in range(n_slots):
        s = jnp.dot(q, kbuf[slot].astype(BF16), preferred_element_type=F32) * scale
        logits.append(s + jnp.where(tok_row == slot // MOBA_TOPK, 0.0, NEG))
    m = jnp.max(s_own, axis=1, keepdims=True)
    for s in logits:
        m = jnp.maximum(m, jnp.max(s, axis=1, keepdims=True))
    p_own = jnp.exp(s_own - m)
    den = jnp.sum(p_own, axis=1, keepdims=True)
    acc = lax.dot_general(p_own.astype(BF16), vnT_ref[...], _NT, preferred_element_type=F32)
    for slot, s in enumerate(logits):
        p = jnp.exp(s - m)
        den = den + jnp.sum(p, axis=1, keepdims=True)
        acc = acc + lax.dot_general(p.astype(BF16), vbuf[slot].astype(BF16), _NT, preferred_element_type=F32)
    o_ref[...] = (acc / den).astype(BF16)


def _moba_gather(pt_flat, idx_flat, q, knT, vnT, ckT, cvT, n_pages):
    batch, _, n_tok, _ = q.shape
    n_slots = n_tok * MOBA_TOPK
    per_bh = lambda r, c: pl.BlockSpec((None, None, r, c), lambda b, h, pt, ix: (b, h, 0, 0))
    return pl.pallas_call(
        functools.partial(_moba_gather_body, n_tok=n_tok, n_pages=n_pages),
        out_shape=jax.ShapeDtypeStruct((batch, N_HEADS, n_tok, HEAD_DIM), BF16),
        grid_spec=pltpu.PrefetchScalarGridSpec(
            num_scalar_prefetch=2,
            grid=(batch, N_HEADS),
            in_specs=[per_bh(n_tok, HEAD_DIM), per_bh(HEAD_DIM, PAGE_SIZE), per_bh(HEAD_DIM, PAGE_SIZE),
                      pl.BlockSpec(memory_space=pl.ANY), pl.BlockSpec(memory_space=pl.ANY)],
            out_specs=per_bh(n_tok, HEAD_DIM),
            scratch_shapes=[pltpu.VMEM((2, n_slots, HEAD_DIM, MOBA_BLOCK), F32),
                            pltpu.VMEM((2, n_slots, HEAD_DIM, MOBA_BLOCK), F32),
                            pltpu.SemaphoreType.DMA((2, 2))],
        ),
        compiler_params=_cparams(("arbitrary", "arbitrary")),
        name="moba_gather",
    )(pt_flat, idx_flat, q, knT, vnT, ckT, cvT)


def _block_diag_q(q, n_tok):
    batch = q.shape[0] // n_tok
    q3 = q.reshape(batch, 1, n_tok, W_ATT)
    col_head = (jnp.arange(W_ATT) // HEAD_DIM)[None, None, None, :]
    row_head = jnp.arange(N_HEADS)[None, :, None, None]
    return jnp.where(col_head == row_head, q3, 0.0).reshape(batch, N_HEADS * n_tok, W_ATT)


def _new_tokens_T(x, n_tok):
    batch = x.shape[0] // n_tok
    xt = x.reshape(batch, n_tok, W_ATT).transpose(0, 2, 1)
    return jnp.pad(xt, ((0, 0), (0, 0), (0, PAGE_SIZE - n_tok)))


def _cache_T(c):
    return c.transpose(0, 2, 3, 1)


def kernel(x_prompt, x_sample, cache_k_moba, cache_v_moba, cache_k_sb, cache_v_sb, cache_mem_k, cache_mem_v,
           page_table, mem_prompt, norm_ffn1, w_ffn1_gu, w_ffn1_down, norm_mix, w_in, norm_mem, w_mem_kv,
           w_br_moba, w_br_sb, w_br_mem, w_out, norm_ffn2, w_ffn2_gu, w_ffn2_down, norm_final):
    batch, seq, d = x_prompt.shape
    dec_batch, n_tok, _ = x_sample.shape
    n_pages = page_table.shape[1]
    past = n_pages * PAGE_SIZE
    depth = w_in.shape[0]
    assert depth == 1 and past % MOBA_BLOCK == 0 and seq % TOKEN_TILE == 0

    lyr = 0
    bf = lambda w: w[lyr].astype(BF16)
    row = lambda g: g.reshape(1, d)
    w1gu, w1d, w2gu, w2d = bf(w_ffn1_gu), bf(w_ffn1_down), bf(w_ffn2_gu), bf(w_ffn2_down)
    win, wmem, wa, wb, wm, wo = bf(w_in), bf(w_mem_kv), bf(w_br_moba), bf(w_br_sb), bf(w_br_mem), bf(w_out)
    n1, nmix, nmem, n2, nfin = row(norm_ffn1[lyr]), row(norm_mix[lyr]), row(norm_mem[lyr]), row(norm_ffn2[lyr]), row(norm_final)

    n = batch * seq
    nb = seq // MOBA_BLOCK
    xp = _ffn(x_prompt.reshape(n, d), n1, w1gu, w1d, nfin, final_norm=False, tm=1024)
    (qaT, kaT, vaT, ka_nat, vaT_blk, kbar, qbT, kbT, vbT, kb_nat, vbT_blk, qm, gates) = _proj_prompt(
        xp, nmix, win, _rope_tables(jnp.arange(seq)), batch, seq)
    mem_kv = _normproj(mem_prompt.reshape(-1, d), nmem, wmem, tm=512)
    n_mem = mem_prompt.shape[1]
    mk = mem_kv[:, :W_MEM].reshape(batch, n_mem, W_MEM)
    mv = mem_kv[:, W_MEM:].reshape(batch, n_mem, W_MEM)
    oaT = _moba_prompt(qaT, ka_nat.reshape(batch, seq, W_ATT), vaT_blk, kbar.reshape(batch, nb, W_ATT))
    obT = _sb_prompt(qbT, kb_nat.reshape(batch, seq, W_ATT), vbT_blk)
    om = _mem_attend(qm.reshape(batch, seq, W_MEM), mk, mv, tq=512)
    nat = lambda t: t.transpose(0, 2, 1).reshape(n, W_ATT)
    xp = _merge(xp, nat(oaT), nat(obT), om.reshape(n, W_MEM), gates, wa, wb, wm, wo, tm=512)
    y_prompt = _ffn(xp, n2, w2gu, w2d, nfin, final_norm=True, tm=1024).reshape(batch, seq, d)
    rows_out = lambda t: t.reshape(batch, N_HEADS, HEAD_DIM, seq).transpose(0, 3, 1, 2)[None]
    mem_out = lambda t: t.reshape(1, batch, n_mem, N_HEADS_MEM, HEAD_DIM_MEM)

    ns = dec_batch * n_tok
    xs = _ffn(x_sample.reshape(ns, d), n1, w1gu, w1d, nfin, final_norm=False, tm=ns)
    pos_s = jnp.tile(past + jnp.arange(n_tok), dec_batch)
    qa, ka, va, qb, kb, vb, qm_s, gates_s = _proj_sample(xs, nmix, win, _rope_tables(pos_s))

    ck_a, cv_a = _cache_T(cache_k_moba[lyr]), _cache_T(cache_v_moba[lyr])
    ck_b, cv_b = _cache_T(cache_k_sb[lyr]), _cache_T(cache_v_sb[lyr])
    pool = ck_a.shape[0]
    flat = lambda c: c.reshape(pool, W_ATT, PAGE_SIZE)

    idx = _moba_select(page_table, _block_diag_q(qa, n_tok).astype(BF16), flat(ck_a))
    heads = lambda t: t.reshape(dec_batch, n_tok, N_HEADS, HEAD_DIM).transpose(0, 2, 1, 3)
    heads_T = lambda t: _new_tokens_T(t, n_tok).reshape(dec_batch, N_HEADS, HEAD_DIM, PAGE_SIZE)
    oa_s = _moba_gather(page_table.reshape(-1), idx[:, :, :MOBA_TOPK].reshape(-1),
                        heads(qa).astype(BF16), heads_T(ka).astype(BF16), heads_T(va).astype(BF16),
                        ck_a, cv_a, n_pages)
    oa_s = oa_s.transpose(0, 2, 1, 3).reshape(ns, W_ATT)
    ob_s = _sb_decode(page_table, _block_diag_q(qb, n_tok).astype(BF16),
                      _new_tokens_T(kb, n_tok).astype(BF16), _new_tokens_T(vb, n_tok).astype(BF16),
                      flat(ck_b), flat(cv_b), n_tok).reshape(ns, W_ATT)
    om_s = _mem_attend(qm_s.reshape(dec_batch, n_tok, W_MEM),
                       cache_mem_k[lyr].reshape(dec_batch, -1, W_MEM),
                       cache_mem_v[lyr].reshape(dec_batch, -1, W_MEM), tq=n_tok).reshape(ns, W_MEM)
    xs = _merge(xs, oa_s, ob_s, om_s, gates_s, wa, wb, wm, wo, tm=ns)
    y_sample = _ffn(xs, n2, w2gu, w2d, nfin, final_norm=True, tm=ns).reshape(dec_batch, n_tok, d)
    new_rows = lambda t: t.reshape(1, dec_batch, n_tok, N_HEADS, HEAD_DIM)

    return (y_prompt, y_sample, rows_out(kaT), rows_out(vaT), rows_out(kbT), rows_out(vbT),
            mem_out(mk), mem_out(mv), new_rows(ka), new_rows(va), new_rows(kb), new_rows(vb))
```

```python
import functools

import jax
import jax.numpy as jnp
from jax import lax
from jax.experimental import pallas as pl
from jax.experimental.pallas import tpu as pltpu

F32 = jnp.float32
BF16 = jnp.bfloat16

D_MODEL = 1024
HEAD_DIM = 64
N_HEADS = 8
W_ATT = N_HEADS * HEAD_DIM
N_HEADS_MEM = 4
HEAD_DIM_MEM = 128
W_MEM = N_HEADS_MEM * HEAD_DIM_MEM
N_BRANCH = 3
D_IN = 6 * W_ATT + W_MEM + N_BRANCH * D_MODEL
D_FF = 2816
MOBA_BLOCK = 256
MOBA_TOPK = 3
PAGE_SIZE = 128
ROT_DIM = HEAD_DIM // 4
ROPE_THETA = 500000.0
EPS = 1e-6

NEG = -1e30
LOG_F32_TINY = -88.0
LOG2_E = 1.4426950408889634
FF_CHUNK = 256
TOKEN_TILE = 256
LANES = 128
VMEM_LIMIT_BYTES = 56 * 1024 * 1024

_NT = (((1,), (1,)), ((), ()))


def _cparams(sem):
    return pltpu.CompilerParams(dimension_semantics=sem, vmem_limit_bytes=VMEM_LIMIT_BYTES)


def _rms(x, g):
    return x * lax.rsqrt(jnp.mean(x * x, axis=-1, keepdims=True) + EPS) * g


def _softplus_pair(z):
    t = jnp.log(1.0 + jnp.exp2(jnp.abs(z) * -LOG2_E))
    return jnp.maximum(z, 0.0) + t, jnp.minimum(z, 0.0) - t


def _log_sigmoid_pair(z):
    sp, lsig = _softplus_pair(z)
    return -sp, lsig


def _ffn_body(x_ref, g_ref, wgu_ref, wd_ref, gf_ref, o_ref, act_ref, *, final_norm):
    h = _rms(x_ref[...], g_ref[...]).astype(BF16)
    for c in range(D_FF // FF_CHUNK):
        cols = slice(c * FF_CHUNK, (c + 1) * FF_CHUNK)
        up_cols = slice(D_FF + c * FF_CHUNK, D_FF + (c + 1) * FF_CHUNK)
        gate = jnp.dot(h, wgu_ref[:, cols], preferred_element_type=F32)
        up = jnp.dot(h, wgu_ref[:, up_cols], preferred_element_type=F32)
        act_ref[:, cols] = (gate * jax.nn.sigmoid(gate) * up).astype(BF16)
    y = x_ref[...] + 0.5 * jnp.dot(act_ref[...], wd_ref[...], preferred_element_type=F32)
    if final_norm:
        y = _rms(y, gf_ref[...])
    o_ref[...] = y


def _ffn(x, g, w_gu, w_down, g_final, *, final_norm, tm):
    n, d = x.shape
    resident = lambda a: pl.BlockSpec(a.shape, lambda i: (0, 0), pipeline_mode=pl.Buffered(1))
    return pl.pallas_call(
        functools.partial(_ffn_body, final_norm=final_norm),
        out_shape=jax.ShapeDtypeStruct((n, d), F32),
        grid=(n // tm,),
        in_specs=[
            pl.BlockSpec((tm, d), lambda i: (i, 0)),
            pl.BlockSpec((1, d), lambda i: (0, 0)),
            resident(w_gu),
            resident(w_down),
            pl.BlockSpec((1, d), lambda i: (0, 0)),
        ],
        out_specs=pl.BlockSpec((tm, d), lambda i: (i, 0)),
        scratch_shapes=[pltpu.VMEM((tm, D_FF), BF16)],
        compiler_params=_cparams(("parallel",)),
        name="ffn",
    )(x, g, w_gu, w_down, g_final)


def _rope_tables(pos):
    half = ROT_DIM // 2
    inv = ROPE_THETA ** (-(jnp.arange(half, dtype=F32) * 2.0) / ROT_DIM)
    ang = pos.astype(F32)[:, None] * inv[None, :]
    cos, sin = jnp.cos(ang), jnp.sin(ang)
    t = pos.shape[0]
    rest = HEAD_DIM - ROT_DIM
    z8 = jnp.zeros((t, half), F32)
    cos_h = jnp.concatenate([cos, cos, jnp.ones((t, rest), F32)], axis=1)
    sa_h = jnp.concatenate([-sin, z8, jnp.zeros((t, rest), F32)], axis=1)
    sb_h = jnp.concatenate([z8, sin, jnp.zeros((t, rest), F32)], axis=1)
    two = lambda a: jnp.concatenate([a, a], axis=1)
    return two(cos_h), two(sa_h), two(sb_h)


def _rope(z, cos, sa, sb):
    half = ROT_DIM // 2
    w = z.shape[1]
    return z * cos + pltpu.roll(z, w - half, 1) * sa + pltpu.roll(z, half, 1) * sb


def _proj_common(x_ref, g_ref, w_ref, cos_ref, sa_ref, sb_ref):
    h = _rms(x_ref[...], g_ref[...]).astype(BF16)
    reps = W_ATT // LANES
    wide = lambda r: jnp.concatenate([r[...]] * reps, axis=1)
    cos, sa, sb = wide(cos_ref), wide(sa_ref), wide(sb_ref)

    def col(c):
        return jnp.dot(h, w_ref[:, c * W_ATT:(c + 1) * W_ATT], preferred_element_type=F32)

    rope = lambda z: _rope(z, cos, sa, sb)
    return col, rope


def _store_gates(col, gt_ref):
    for c in range(N_BRANCH * D_MODEL // W_ATT):
        gt_ref[:, c * W_ATT:(c + 1) * W_ATT] = jax.nn.sigmoid(col(7 + c)).astype(BF16)


def _proj_prompt_body(x_ref, g_ref, w_ref, cos_ref, sa_ref, sb_ref,
                      qaT_ref, kaT_ref, vaT_ref, kan_ref, vaTb_ref, kbar_ref,
                      qbT_ref, kbT_ref, vbT_ref, kbn_ref, vbTb_ref, qm_ref, gt_ref):
    col, rope = _proj_common(x_ref, g_ref, w_ref, cos_ref, sa_ref, sb_ref)
    qaT_ref[...] = rope(col(0)).T.astype(BF16)
    ka = rope(col(1))
    kaT_ref[...] = ka.T
    kan_ref[...] = ka.astype(BF16)
    kbar_ref[...] = jnp.sum(ka, axis=0, keepdims=True) * (1.0 / MOBA_BLOCK)
    vaT = col(2).T
    vaT_ref[...] = vaT
    vaTb_ref[...] = vaT.astype(BF16)
    qbT_ref[...] = col(3).T.astype(BF16)
    kb = col(4)
    kbT_ref[...] = kb.T
    kbn_ref[...] = kb.astype(BF16)
    vbT = col(5).T
    vbT_ref[...] = vbT
    vbTb_ref[...] = vbT.astype(BF16)
    qm_ref[...] = col(6).astype(BF16)
    _store_gates(col, gt_ref)


def _proj_sample_body(x_ref, g_ref, w_ref, cos_ref, sa_ref, sb_ref,
                      qa_ref, ka_ref, va_ref, qb_ref, kb_ref, vb_ref, qm_ref, gt_ref):
    col, rope = _proj_common(x_ref, g_ref, w_ref, cos_ref, sa_ref, sb_ref)
    qa_ref[...] = rope(col(0))
    ka_ref[...] = rope(col(1))
    va_ref[...] = col(2)
    qb_ref[...] = col(3)
    kb_ref[...] = col(4)
    vb_ref[...] = col(5)
    qm_ref[...] = col(6).astype(BF16)
    _store_gates(col, gt_ref)


def _proj_in_specs(tm, n_tab_tiles):
    tab = pl.BlockSpec((tm, LANES), lambda i: (i % n_tab_tiles, 0))
    return [
        pl.BlockSpec((tm, D_MODEL), lambda i: (i, 0)),
        pl.BlockSpec((1, D_MODEL), lambda i: (0, 0)),
        pl.BlockSpec((D_MODEL, D_IN), lambda i: (0, 0), pipeline_mode=pl.Buffered(1)),
        tab, tab, tab,
    ]


def _proj_prompt(x, g, w_in, tables, batch, seq):
    tm = TOKEN_TILE
    nb = seq // tm
    n = batch * seq
    sds = jax.ShapeDtypeStruct
    t_spec = pl.BlockSpec((None, W_ATT, tm), lambda i: (i // nb, 0, i % nb))
    blk_spec = pl.BlockSpec((None, None, W_ATT, tm), lambda i: (i // nb, i % nb, 0, 0))
    nat_spec = pl.BlockSpec((tm, W_ATT), lambda i: (i, 0))
    t_bf = sds((batch, W_ATT, seq), BF16)
    t_f32 = sds((batch, W_ATT, seq), F32)
    blk_bf = sds((batch, nb, W_ATT, tm), BF16)
    nat_bf = sds((n, W_ATT), BF16)
    out_shape = [t_bf, t_f32, t_f32, nat_bf, blk_bf, sds((n // tm, 1, W_ATT), F32),
                 t_bf, t_f32, t_f32, nat_bf, blk_bf, nat_bf, sds((n, N_BRANCH * D_MODEL), BF16)]
    out_specs = [t_spec, t_spec, t_spec, nat_spec, blk_spec,
                 pl.BlockSpec((None, 1, W_ATT), lambda i: (i, 0, 0)),
                 t_spec, t_spec, t_spec, nat_spec, blk_spec, nat_spec,
                 pl.BlockSpec((tm, N_BRANCH * D_MODEL), lambda i: (i, 0))]
    return pl.pallas_call(
        _proj_prompt_body,
        out_shape=out_shape,
        grid=(n // tm,),
        in_specs=_proj_in_specs(tm, nb),
        out_specs=out_specs,
        compiler_params=_cparams(("parallel",)),
        name="proj_prompt",
    )(x, g, w_in, *tables)


def _proj_sample(x, g, w_in, tables):
    n = x.shape[0]
    sds = jax.ShapeDtypeStruct
    nat = lambda dt: sds((n, W_ATT), dt)
    spec = pl.BlockSpec((n, W_ATT), lambda i: (0, 0))
    out_shape = [nat(F32)] * 6 + [nat(BF16), sds((n, N_BRANCH * D_MODEL), BF16)]
    out_specs = [spec] * 7 + [pl.BlockSpec((n, N_BRANCH * D_MODEL), lambda i: (0, 0))]
    return pl.pallas_call(
        _proj_sample_body,
        out_shape=out_shape,
        grid=(1,),
        in_specs=_proj_in_specs(n, 1),
        out_specs=out_specs,
        compiler_params=_cparams(("arbitrary",)),
        name="proj_sample",
    )(x, g, w_in, *tables)


def _normproj_body(x_ref, g_ref, w_ref, o_ref):
    h = _rms(x_ref[...], g_ref[...]).astype(BF16)
    o_ref[...] = jnp.dot(h, w_ref[...], preferred_element_type=F32)


def _normproj(x, g, w, tm):
    n, d = x.shape
    dout = w.shape[1]
    return pl.pallas_call(
        _normproj_body,
        out_shape=jax.ShapeDtypeStruct((n, dout), F32),
        grid=(n // tm,),
        in_specs=[pl.BlockSpec((tm, d), lambda i: (i, 0)),
                  pl.BlockSpec((1, d), lambda i: (0, 0)),
                  pl.BlockSpec((d, dout), lambda i: (0, 0))],
        out_specs=pl.BlockSpec((tm, dout), lambda i: (i, 0)),
        compiler_params=_cparams(("parallel",)),
        name="mem_kv",
    )(x, g, w)


def _head_masked(qT):
    row = lax.broadcasted_iota(jnp.int32, qT.shape, 0)
    zero = jnp.zeros_like(qT)
    return [jnp.where(row < HEAD_DIM, qT, zero), jnp.where(row >= HEAD_DIM, qT, zero)]


MOBA_SUM_ROWS = 16
MOBA_PAIRS_PER_STEP = 2
SB_PAIRS_PER_STEP = 2


def _moba_prompt_body(qT_ref, k_ref, vT_ref, kbar_ref, oT_ref, sel_ref, m_ref, pv_ref,
                      sa_ref, sb_ref, pa_ref, pb_ref, *, nb):
    qi = pl.program_id(2)
    tq = qT_ref.shape[1]
    pw = 2 * HEAD_DIM
    chains = range(MOBA_PAIRS_PER_STEP)
    pair_rows = lambda c: slice(c * pw, (c + 1) * pw)
    nidx = lax.broadcasted_iota(jnp.int32, (nb, 2 * tq), 0)
    qs = []
    for c in chains:
        q2 = jnp.concatenate(_head_masked(qT_ref[pair_rows(c), :]), axis=1)
        qs.append(q2 * jnp.asarray(HEAD_DIM ** -0.5, BF16))
        sc = jnp.dot(kbar_ref[:, pair_rows(c)].astype(BF16), q2, preferred_element_type=F32)
        sc = jnp.where(nidx < qi, sc, -jnp.inf)
        rank = jnp.zeros((nb, 2 * tq), F32)
        for m in range(nb):
            r = sc[m:m + 1, :]
            beats = (r > sc) | ((r == sc) & (m < nidx))
            rank = rank + jnp.where(beats, 1.0, 0.0)
        sel_ref[c] = jnp.where((nidx < qi) & (rank < MOBA_TOPK), 0.0, NEG)

    def key_block(c, n):
        return k_ref[pl.ds(pl.multiple_of(n * MOBA_BLOCK, MOBA_BLOCK), MOBA_BLOCK), pair_rows(c)]

    def past_logits(c, e):
        n = e - 1
        return jnp.dot(key_block(c, n), qs[c], preferred_element_type=F32) + sel_ref[c, pl.ds(n, 1), :]

    def weights_into(c, e, s_ref, p_ref):
        s = s_ref[c]
        m = jnp.max(s, axis=0, keepdims=True)
        m_ref[c, pl.ds(e, 1), :] = m
        p_ref[c] = jnp.exp((s - m).astype(BF16))

    ones_rows = jnp.ones((MOBA_SUM_ROWS, MOBA_BLOCK), BF16)

    def pv_into(c, slot, p_ref, v_idx):
        pb = p_ref[c]
        vn = vT_ref[v_idx, pair_rows(c), :]
        lhs = [jnp.concatenate([vn[hh * HEAD_DIM:(hh + 1) * HEAD_DIM, :], ones_rows], axis=0) for hh in range(2)]
        pv_ref[c, slot] = jnp.concatenate(
            [jnp.dot(lhs[0], pb[:, :tq], preferred_element_type=F32),
             jnp.dot(lhs[1], pb[:, tq:], preferred_element_type=F32)], axis=1)

    def value_block(e):
        return jnp.where(e == 0, qi, e - 1)

    kpos = lax.broadcasted_iota(jnp.int32, (MOBA_BLOCK, 2 * tq), 0)
    qpos = lax.broadcasted_iota(jnp.int32, (MOBA_BLOCK, 2 * tq), 1) % tq
    causal = jnp.where(kpos <= qpos, 0.0, NEG)
    for c in chains:
        sa_ref[c] = jnp.dot(key_block(c, qi), qs[c], preferred_element_type=F32) + causal
    pb_ref[...] = jnp.zeros_like(pb_ref)
    m_ref[...] = jnp.full(m_ref.shape, NEG, F32)
    spare = nb

    def pair(g, carry):
        e = 2 * g
        for c in chains:
            pv_into(c, jnp.where(e == 0, spare, e - 1), pb_ref, jnp.where(e == 0, 0, value_block(e - 1)))
        for c in chains:
            sb_ref[c] = past_logits(c, e + 1)
        for c in chains:
            weights_into(c, e, sa_ref, pa_ref)
        for c in chains:
            pv_into(c, e, pa_ref, value_block(e))
        for c in chains:
            sa_ref[c] = past_logits(c, e + 2)
        for c in chains:
            weights_into(c, e + 1, sb_ref, pb_ref)
        return carry

    n_pairs = qi // 2 + 1
    lax.fori_loop(0, n_pairs, pair, 0)
    n_elem = 2 * n_pairs
    for c in chains:
        pv_into(c, n_elem - 1, pb_ref, n_elem - 2)

    m_all = [jnp.max(m_ref[c], axis=0, keepdims=True) for c in chains]

    keep = HEAD_DIM + 8
    for c in chains:
        def merge(e, acc, c=c):
            return acc + jnp.exp(m_ref[c, pl.ds(e, 1), :] - m_all[c]) * pv_ref[c, e, :keep, :]

        acc = lax.fori_loop(0, n_elem, merge, jnp.zeros((keep, 2 * tq), F32))
        o = acc[:HEAD_DIM, :] / acc[HEAD_DIM:HEAD_DIM + 1, :]
        oT_ref[pair_rows(c), :] = jnp.concatenate([o[:, :tq], o[:, tq:]], axis=0).astype(BF16)


def _sb_prompt_body(qT_ref, k_ref, vT_ref, oT_ref):
    qi = pl.program_id(2)
    tq = qT_ref.shape[1]
    pw = 2 * HEAD_DIM
    chains = range(SB_PAIRS_PER_STEP)
    pair_rows = lambda c: slice(c * pw, (c + 1) * pw)
    qs = [jnp.concatenate(_head_masked(qT_ref[pair_rows(c), :]), axis=1) * jnp.asarray(HEAD_DIM ** -0.5, BF16)
          for c in chains]
    kk = lax.broadcasted_iota(jnp.int32, (MOBA_BLOCK, MOBA_BLOCK), 0)
    jj = lax.broadcasted_iota(jnp.int32, (MOBA_BLOCK, MOBA_BLOCK), 1)
    neg_later = jnp.where(kk < jj, -1.0, 0.0).astype(BF16)
    kpos = lax.broadcasted_iota(jnp.int32, (MOBA_BLOCK, 2 * tq), 0)
    qpos = lax.broadcasted_iota(jnp.int32, (MOBA_BLOCK, 2 * tq), 1) % tq

    def blocks_terms(blocks):
        zs = [jnp.dot(k_ref[pl.ds(pl.multiple_of(n * MOBA_BLOCK, MOBA_BLOCK), MOBA_BLOCK), pair_rows(c)], qs[c],
                      preferred_element_type=F32) for c, n, _ in blocks]
        pairs = [_softplus_pair(z) for z in zs]
        sps = [sp if vis is None else jnp.where(vis, sp, 0.0) for (sp, _), (_, _, vis) in zip(pairs, blocks)]
        rs = [jnp.dot(neg_later, sp.astype(BF16), preferred_element_type=F32) for sp in sps]
        out = []
        for (c, n, vis), (_, lsig), sp, r in zip(blocks, pairs, sps, rs):
            a = jnp.exp(lsig + r)
            if vis is not None:
                a = jnp.where(vis, a, 0.0)
            ab = a.astype(BF16)
            vn = vT_ref[n, pair_rows(c), :]
            pv = jnp.concatenate(
                [jnp.dot(vn[:HEAD_DIM, :], ab[:, :tq], preferred_element_type=F32),
                 jnp.dot(vn[HEAD_DIM:, :], ab[:, tq:], preferred_element_type=F32)], axis=1)
            out.append((pv, r[0:1, :] - sp[0:1, :]))
        return out

    strict = kpos < qpos
    prev = jnp.maximum(qi - 1, 0)
    terms = blocks_terms([(c, qi, strict) for c in chains] + [(c, prev, None) for c in chains])
    has_prev = qi > 0
    n_ch = len(chains)
    state = []
    for c in chains:
        (pv_d, tot_d), (pv_p, tot_p) = terms[c], terms[n_ch + c]
        state.append(tot_d + jnp.where(has_prev, tot_p, 0.0))
        state.append(pv_d + jnp.where(has_prev, jnp.exp(tot_d), 0.0) * pv_p)

    def alive_flag(st):
        top = st[0]
        for c in chains[1:]:
            top = jnp.maximum(top, st[2 * c])
        return (jnp.max(top) > LOG_F32_TINY).astype(jnp.int32)

    def cond(carry):
        return jnp.logical_and(carry[0] >= 0, carry[1] > 0)

    def body(carry):
        n, _, st = carry
        terms = blocks_terms([(c, n, None) for c in chains])
        new = []
        for c in chains:
            pv, tot = terms[c]
            new.append(st[2 * c] + tot)
            new.append(st[2 * c + 1] + jnp.exp(st[2 * c]) * pv)
        return n - 1, alive_flag(new), tuple(new)

    fin = lax.while_loop(cond, body, (qi - 2, alive_flag(state), tuple(state)))[2]
    for c in chains:
        o = fin[2 * c + 1]
        oT_ref[pair_rows(c), :] = jnp.concatenate([o[:, :tq], o[:, tq:]], axis=0).astype(BF16)


def _prompt_mixer_specs(seq, nb, pairs):
    tq = TOKEN_TILE
    w = pairs * 2 * HEAD_DIM
    q_spec = pl.BlockSpec((None, w, tq), lambda b, hp, qi: (b, hp, qi))
    k_spec = pl.BlockSpec((None, seq, w), lambda b, hp, qi: (b, 0, hp))
    v_spec = pl.BlockSpec((None, nb, w, MOBA_BLOCK), lambda b, hp, qi: (b, 0, hp, 0))
    return q_spec, k_spec, v_spec


def _moba_prompt(qT, k_nat, vT_blk, kbar):
    batch, _, seq = qT.shape
    nb = seq // MOBA_BLOCK
    pairs = MOBA_PAIRS_PER_STEP
    q_spec, k_spec, v_spec = _prompt_mixer_specs(seq, nb, pairs)
    cols = 2 * TOKEN_TILE
    return pl.pallas_call(
        functools.partial(_moba_prompt_body, nb=nb),
        out_shape=jax.ShapeDtypeStruct(qT.shape, BF16),
        grid=(batch, N_HEADS // (2 * pairs), seq // TOKEN_TILE),
        in_specs=[q_spec, k_spec, v_spec,
                  pl.BlockSpec((None, nb, pairs * 2 * HEAD_DIM), lambda b, hp, qi: (b, 0, hp))],
        out_specs=q_spec,
        scratch_shapes=[pltpu.VMEM((pairs, nb, cols), F32)] * 2
        + [pltpu.VMEM((pairs, nb + 1, HEAD_DIM + MOBA_SUM_ROWS, cols), F32)]
        + [pltpu.VMEM((pairs, MOBA_BLOCK, cols), F32)] * 2
        + [pltpu.VMEM((pairs, MOBA_BLOCK, cols), BF16)] * 2,
        compiler_params=_cparams(("parallel", "parallel", "arbitrary")),
        name="moba_prompt",
    )(qT, k_nat, vT_blk, kbar)


def _sb_prompt(qT, k_nat, vT_blk):
    batch, _, seq = qT.shape
    nb = seq // MOBA_BLOCK
    q_spec, k_spec, v_spec = _prompt_mixer_specs(seq, nb, SB_PAIRS_PER_STEP)
    return pl.pallas_call(
        _sb_prompt_body,
        out_shape=jax.ShapeDtypeStruct(qT.shape, BF16),
        grid=(batch, N_HEADS // (2 * SB_PAIRS_PER_STEP), seq // TOKEN_TILE),
        in_specs=[q_spec, k_spec, v_spec],
        out_specs=q_spec,
        compiler_params=_cparams(("parallel", "parallel", "arbitrary")),
        name="sb_prompt",
    )(qT, k_nat, vT_blk)


def _mem_attend_body(q_ref, k_ref, v_ref, o_ref):
    q = q_ref[...]
    k = k_ref[...].astype(BF16)
    v = v_ref[...].astype(BF16)
    scale = HEAD_DIM_MEM ** -0.5
    outs = []
    for h in range(N_HEADS_MEM):
        sl = slice(h * HEAD_DIM_MEM, (h + 1) * HEAD_DIM_MEM)
        s = lax.dot_general(q[:, sl], k[:, sl], _NT, preferred_element_type=F32) * scale
        p = jnp.exp(s - jnp.max(s, axis=1, keepdims=True))
        o = jnp.dot(p.astype(BF16), v[:, sl], preferred_element_type=F32)
        outs.append(o / jnp.sum(p, axis=1, keepdims=True))
    o_ref[...] = jnp.concatenate(outs, axis=1).astype(BF16)


def _mem_attend(q, k, v, tq):
    batch, seq, w = q.shape
    n_mem = k.shape[1]
    kv_spec = pl.BlockSpec((None, n_mem, w), lambda b, i: (b, 0, 0))
    q_spec = pl.BlockSpec((None, tq, w), lambda b, i: (b, i, 0))
    return pl.pallas_call(
        _mem_attend_body,
        out_shape=jax.ShapeDtypeStruct(q.shape, BF16),
        grid=(batch, seq // tq),
        in_specs=[q_spec, kv_spec, kv_spec],
        out_specs=q_spec,
        compiler_params=_cparams(("parallel", "parallel")),
        name="mem_attend",
    )(q, k, v)


def _merge_body(x_ref, oa_ref, ob_ref, om_ref, gt_ref, wa_ref, wb_ref, wm_ref, wo_ref, o_ref):
    merged = None
    for i, (o, w) in enumerate(((oa_ref, wa_ref), (ob_ref, wb_ref), (om_ref, wm_ref))):
        br = jnp.dot(o[...], w[...], preferred_element_type=F32)
        term = gt_ref[:, i * D_MODEL:(i + 1) * D_MODEL].astype(F32) * br
        merged = term if merged is None else merged + term
    o_ref[...] = x_ref[...] + jnp.dot(merged.astype(BF16), wo_ref[...], preferred_element_type=F32)


def _merge(x, oa, ob, om, gates, wa, wb, wm, wo, tm):
    n, d = x.shape
    row = lambda w: pl.BlockSpec((tm, w), lambda i: (i, 0))
    full = lambda a: pl.BlockSpec(a.shape, lambda i: (0, 0))
    return pl.pallas_call(
        _merge_body,
        out_shape=jax.ShapeDtypeStruct((n, d), F32),
        grid=(n // tm,),
        in_specs=[row(d), row(W_ATT), row(W_ATT), row(W_MEM), row(N_BRANCH * d),
                  full(wa), full(wb), full(wm), full(wo)],
        out_specs=row(d),
        compiler_params=_cparams(("parallel",)),
        name="merge",
    )(x, oa, ob, om, gates, wa, wb, wm, wo)


SELECT_PAGES_PER_STEP = 32


def _own_head_columns(acc, n_tok):
    lane_head = lax.broadcasted_iota(jnp.int32, (n_tok, W_ATT), 1) // HEAD_DIM
    out = jnp.zeros((n_tok, W_ATT), F32)
    for h in range(N_HEADS):
        out = out + jnp.where(lane_head == h, acc[h * n_tok:(h + 1) * n_tok, :], 0.0)
    return out


def _suffix_sum_lanes(x):
    lane = lax.broadcasted_iota(jnp.int32, x.shape, 1)
    n = x.shape[1]
    sh = 1
    while sh < n:
        x = x + jnp.where(lane < n - sh, pltpu.roll(x, n - sh, 1), 0.0)
        sh *= 2
    return x


SB_RING = 4


def _sb_decode_body(pt_ref, qbd_ref, knT_ref, vnT_ref, ck_hbm, cv_hbm, o_ref, kbuf, vbuf, sem,
                    *, n_tok, n_pages):
    b = pl.program_id(0)
    qbd = qbd_ref[...] * jnp.asarray(HEAD_DIM ** -0.5, BF16)
    rows = N_HEADS * n_tok
    ahead = SB_RING - 1

    def page_copies(p):
        slot = p % SB_RING
        phys = pt_ref[b, n_pages - 1 - p]
        return (pltpu.make_async_copy(ck_hbm.at[phys], kbuf.at[slot], sem.at[0, slot]),
                pltpu.make_async_copy(cv_hbm.at[phys], vbuf.at[slot], sem.at[1, slot]))

    def start(p):
        for c in page_copies(p):
            c.start()

    def wait(p):
        for c in page_copies(p):
            c.wait()

    def page(kT, vT, lsurv, visible):
        z = jnp.dot(qbd, kT.astype(BF16), preferred_element_type=F32)
        l, ls = _log_sigmoid_pair(z)
        if visible is not None:
            l = jnp.where(visible, l, 0.0)
        suffix = _suffix_sum_lanes(l)
        a = jnp.exp(ls + (suffix - l) + lsurv)
        if visible is not None:
            a = jnp.where(visible, a, 0.0)
        o = lax.dot_general(a.astype(BF16), vT.astype(BF16), _NT, preferred_element_type=F32)
        return o, lsurv + suffix[:, 0:1]

    def alive_flag(lsurv):
        return (jnp.max(lsurv) > LOG_F32_TINY).astype(jnp.int32)

    for p in range(ahead):
        start(p)

    key = lax.broadcasted_iota(jnp.int32, (rows, PAGE_SIZE), 1)
    tok = lax.broadcasted_iota(jnp.int32, (rows, PAGE_SIZE), 0) % n_tok
    acc0, ls0 = page(knT_ref[...], vnT_ref[...], jnp.zeros((rows, 1), F32), key < tok)

    def cond(c):
        return jnp.logical_and(c[0] < n_pages, c[3] > 0)

    def body(c):
        p, lsurv, acc, _ = c
        wait(p)

        @pl.when(p + ahead < n_pages)
        def _():
            start(p + ahead)

        slot = p % SB_RING
        o, lsurv = page(kbuf[slot], vbuf[slot], lsurv, None)
        return p + 1, lsurv, acc + o, alive_flag(lsurv)

    p_end, _, acc, _ = lax.while_loop(cond, body, (0, ls0, acc0, alive_flag(ls0)))

    for j in range(ahead):
        @pl.when(p_end + j < n_pages)
        def _():
            wait(p_end + j)

    o_ref[...] = _own_head_columns(acc, n_tok).astype(BF16)


def _sb_decode(page_table, qbd, knT, vnT, ckT, cvT, n_tok):
    batch, n_pages = page_table.shape
    assert n_pages >= SB_RING
    rows = N_HEADS * n_tok
    per_b = lambda r, c: pl.BlockSpec((None, r, c), lambda b, pt: (b, 0, 0))
    hbm = pl.BlockSpec(memory_space=pl.ANY)
    return pl.pallas_call(
        functools.partial(_sb_decode_body, n_tok=n_tok, n_pages=n_pages),
        out_shape=jax.ShapeDtypeStruct((batch, n_tok, W_ATT), BF16),
        grid_spec=pltpu.PrefetchScalarGridSpec(
            num_scalar_prefetch=1,
            grid=(batch,),
            in_specs=[per_b(rows, W_ATT), per_b(W_ATT, PAGE_SIZE), per_b(W_ATT, PAGE_SIZE), hbm, hbm],
            out_specs=per_b(n_tok, W_ATT),
            scratch_shapes=[pltpu.VMEM((SB_RING, W_ATT, PAGE_SIZE), F32),
                            pltpu.VMEM((SB_RING, W_ATT, PAGE_SIZE), F32),
                            pltpu.SemaphoreType.DMA((2, SB_RING))],
        ),
        compiler_params=_cparams(("arbitrary",)),
        name="sb_decode",
    )(page_table, qbd, knT, vnT, ckT, cvT)


def _moba_select_body(pt_ref, qbd_ref, *refs, n_pages):
    g = SELECT_PAGES_PER_STEP
    k_refs = refs[:g]
    idx_ref, sc_ref = refs[g:]
    step = pl.program_id(1)
    ppb = MOBA_BLOCK // PAGE_SIZE
    qbd = qbd_ref[...]
    rows = qbd.shape[0]
    lane = lax.broadcasted_iota(jnp.int32, (rows, LANES), 1)

    @pl.when(step == 0)
    def _():
        sc_ref[...] = jnp.zeros_like(sc_ref)

    sc = sc_ref[...]
    for i in range(0, g, ppb):
        ksum = k_refs[i][...]
        for j in range(1, ppb):
            ksum = ksum + k_refs[i + j][...]
        z = jnp.dot(qbd, ksum.astype(BF16), preferred_element_type=F32)
        blk = (step * g + i) // ppb
        sc = sc + jnp.where(lane == blk, jnp.sum(z, axis=1, keepdims=True), 0.0)
    sc_ref[...] = sc

    @pl.when(step == n_pages // g - 1)
    def _():
        nb = n_pages // ppb
        s = jnp.where(lane < nb, sc_ref[...], -jnp.inf)
        out = jnp.zeros((rows, LANES), jnp.int32)
        for j in range(MOBA_TOPK):
            best = jnp.max(s, axis=1, keepdims=True)
            pick = jnp.min(jnp.where(s == best, lane, LANES), axis=1, keepdims=True)
            out = jnp.where(lane == j, pick, out)
            s = jnp.where(lane == pick, -jnp.inf, s)
        idx_ref[...] = out


def _moba_select(page_table, qbd, ckT):
    batch, n_pages = page_table.shape
    g = SELECT_PAGES_PER_STEP
    rows = qbd.shape[1]

    def page_spec(i):
        return pl.BlockSpec((None, W_ATT, PAGE_SIZE), lambda b, s, pt: (pt[b, s * g + i], 0, 0))

    return pl.pallas_call(
        functools.partial(_moba_select_body, n_pages=n_pages),
        out_shape=jax.ShapeDtypeStruct((batch, rows, LANES), jnp.int32),
        grid_spec=pltpu.PrefetchScalarGridSpec(
            num_scalar_prefetch=1,
            grid=(batch, n_pages // g),
            in_specs=[pl.BlockSpec((None, rows, W_ATT), lambda b, s, pt: (b, 0, 0))]
            + [page_spec(i) for i in range(g)],
            out_specs=pl.BlockSpec((None, rows, LANES), lambda b, s, pt: (b, 0, 0)),
            scratch_shapes=[pltpu.VMEM((rows, LANES), F32)],
        ),
        compiler_params=_cparams(("parallel", "arbitrary")),
        name="moba_select",
    )(page_table, qbd, *([ckT] * g))


def _moba_gather_body(pt_ref, idx_ref, q_ref, knT_ref, vnT_ref, ck_hbm, cv_hbm, o_ref,
                      kbuf, vbuf, sem, *, n_tok, n_pages):
    ppb = MOBA_BLOCK // PAGE_SIZE
    n_slots = n_tok * MOBA_TOPK
    scale = HEAD_DIM ** -0.5
    step = pl.program_id(0) * N_HEADS + pl.program_id(1)
    n_steps = pl.num_programs(0) * N_HEADS
    half = step % 2

    def start_copies(st, hf):
        bb, hh = st // N_HEADS, st % N_HEADS
        for slot in range(n_slots):
            tok, j = divmod(slot, MOBA_TOPK)
            blk = idx_ref[((bb * N_HEADS + hh) * n_tok + tok) * MOBA_TOPK + j]
            for i in range(ppb):
                pg = pt_ref[bb * n_pages + blk * ppb + i]
                dst = pl.ds(slot * MOBA_BLOCK + i * PAGE_SIZE, PAGE_SIZE)
                pltpu.make_async_copy(ck_hbm.at[pg, hh], kbuf.at[hf, :, dst], sem.at[0, hf]).start()
                pltpu.make_async_copy(cv_hbm.at[pg, hh], vbuf.at[hf, :, dst], sem.at[1, hf]).start()

    @pl.when(step == 0)
    def _():
        start_copies(step, half)

    @pl.when(step + 1 < n_steps)
    def _():
        start_copies(step + 1, 1 - half)

    kbuf = kbuf.at[half]
    vbuf = vbuf.at[half]
    pltpu.make_async_copy(kbuf, kbuf, sem.at[0, half]).wait()
    pltpu.make_async_copy(vbuf, vbuf, sem.at[1, half]).wait()
    q = q_ref[...] * jnp.asarray(scale, BF16)
    key = lax.broadcasted_iota(jnp.int32, (n_tok, PAGE_SIZE), 1)
    tok = lax.broadcasted_iota(jnp.int32, (n_tok, PAGE_SIZE), 0)
    s_own = jnp.dot(q, knT_ref[...], preferred_element_type=F32) + jnp.where(key <= tok, 0.0, NEG)
    width = n_slots * MOBA_BLOCK
    col_tok = lax.broadcasted_iota(jnp.int32, (n_tok, width), 1) // (MOBA_TOPK * MOBA_BLOCK)
    row_tok = lax.broadcasted_iota(jnp.int32, (n_tok, width), 0)
    s_sel = (jnp.dot(q, kbuf[...].astype(BF16), preferred_element_type=F32)
             + jnp.where(col_tok == row_tok, 0.0, NEG))
    m = jnp.maximum(jnp.max(s_own, axis=1, keepdims=True), jnp.max(s_sel, axis=1, keepdims=True))
    p_own = jnp.exp(s_own - m)
    p_sel = jnp.exp(s_sel - m)
    den = jnp.sum(p_own, axis=1, keepdims=True) + jnp.sum(p_sel, axis=1, keepdims=True)
    acc = (lax.dot_general(p_own.astype(BF16), vnT_ref[...], _NT, preferred_element_type=F32)
           + lax.dot_general(p_sel.astype(BF16), vbuf[...].astype(BF16), _NT, preferred_element_type=F32))
    o_ref[...] = (acc / den).astype(BF16)


def _moba_gather(pt_flat, idx_flat, q, knT, vnT, ckT, cvT, n_pages):
    batch, _, n_tok, _ = q.shape
    n_slots = n_tok * MOBA_TOPK
    per_bh = lambda r, c: pl.BlockSpec((None, None, r, c), lambda b, h, pt, ix: (b, h, 0, 0))
    return pl.pallas_call(
        functools.partial(_moba_gather_body, n_tok=n_tok, n_pages=n_pages),
        out_shape=jax.ShapeDtypeStruct((batch, N_HEADS, n_tok, HEAD_DIM), BF16),
        grid_spec=pltpu.PrefetchScalarGridSpec(
            num_scalar_prefetch=2,
            grid=(batch, N_HEADS),
            in_specs=[per_bh(n_tok, HEAD_DIM), per_bh(HEAD_DIM, PAGE_SIZE), per_bh(HEAD_DIM, PAGE_SIZE),
                      pl.BlockSpec(memory_space=pl.ANY), pl.BlockSpec(memory_space=pl.ANY)],
            out_specs=per_bh(n_tok, HEAD_DIM),
            scratch_shapes=[pltpu.VMEM((2, HEAD_DIM, n_slots * MOBA_BLOCK), F32),
                            pltpu.VMEM((2, HEAD_DIM, n_slots * MOBA_BLOCK), F32),
                            pltpu.SemaphoreType.DMA((2, 2))],
        ),
        compiler_params=_cparams(("arbitrary", "arbitrary")),
        name="moba_gather",
    )(pt_flat, idx_flat, q, knT, vnT, ckT, cvT)


def _block_diag_q(q, n_tok):
    batch = q.shape[0] // n_tok
    q3 = q.reshape(batch, 1, n_tok, W_ATT)
    col_head = (jnp.arange(W_ATT) // HEAD_DIM)[None, None, None, :]
    row_head = jnp.arange(N_HEADS)[None, :, None, None]
    return jnp.where(col_head == row_head, q3, 0.0).reshape(batch, N_HEADS * n_tok, W_ATT)


def _new_tokens_T(x, n_tok):
    batch = x.shape[0] // n_tok
    xt = x.reshape(batch, n_tok, W_ATT).transpose(0, 2, 1)
    return jnp.pad(xt, ((0, 0), (0, 0), (0, PAGE_SIZE - n_tok)))


def _cache_T(c):
    return c.transpose(0, 2, 3, 1)


def kernel(x_prompt, x_sample, cache_k_moba, cache_v_moba, cache_k_sb, cache_v_sb, cache_mem_k, cache_mem_v,
           page_table, mem_prompt, norm_ffn1, w_ffn1_gu, w_ffn1_down, norm_mix, w_in, norm_mem, w_mem_kv,
           w_br_moba, w_br_sb, w_br_mem, w_out, norm_ffn2, w_ffn2_gu, w_ffn2_down, norm_final):
    batch, seq, d = x_prompt.shape
    dec_batch, n_tok, _ = x_sample.shape
    n_pages = page_table.shape[1]
    past = n_pages * PAGE_SIZE
    depth = w_in.shape[0]
    assert depth == 1 and past % MOBA_BLOCK == 0 and seq % TOKEN_TILE == 0

    lyr = 0
    bf = lambda w: w[lyr].astype(BF16)
    row = lambda g: g.reshape(1, d)
    w1gu, w1d, w2gu, w2d = bf(w_ffn1_gu), bf(w_ffn1_down), bf(w_ffn2_gu), bf(w_ffn2_down)
    win, wmem, wa, wb, wm, wo = bf(w_in), bf(w_mem_kv), bf(w_br_moba), bf(w_br_sb), bf(w_br_mem), bf(w_out)
    n1, nmix, nmem, n2, nfin = row(norm_ffn1[lyr]), row(norm_mix[lyr]), row(norm_mem[lyr]), row(norm_ffn2[lyr]), row(norm_final)

    n = batch * seq
    nb = seq // MOBA_BLOCK
    xp = _ffn(x_prompt.reshape(n, d), n1, w1gu, w1d, nfin, final_norm=False, tm=1024)
    (qaT, kaT, vaT, ka_nat, vaT_blk, kbar, qbT, kbT, vbT, kb_nat, vbT_blk, qm, gates) = _proj_prompt(
        xp, nmix, win, _rope_tables(jnp.arange(seq)), batch, seq)
    mem_kv = _normproj(mem_prompt.reshape(-1, d), nmem, wmem, tm=512)
    n_mem = mem_prompt.shape[1]
    mk = mem_kv[:, :W_MEM].reshape(batch, n_mem, W_MEM)
    mv = mem_kv[:, W_MEM:].reshape(batch, n_mem, W_MEM)
    oaT = _moba_prompt(qaT, ka_nat.reshape(batch, seq, W_ATT), vaT_blk, kbar.reshape(batch, nb, W_ATT))
    obT = _sb_prompt(qbT, kb_nat.reshape(batch, seq, W_ATT), vbT_blk)
    om = _mem_attend(qm.reshape(batch, seq, W_MEM), mk, mv, tq=512)
    nat = lambda t: t.transpose(0, 2, 1).reshape(n, W_ATT)
    xp = _merge(xp, nat(oaT), nat(obT), om.reshape(n, W_MEM), gates, wa, wb, wm, wo, tm=512)
    y_prompt = _ffn(xp, n2, w2gu, w2d, nfin, final_norm=True, tm=1024).reshape(batch, seq, d)
    rows_out = lambda t: t.reshape(batch, N_HEADS, HEAD_DIM, seq).transpose(0, 3, 1, 2)[None]
    mem_out = lambda t: t.reshape(1, batch, n_mem, N_HEADS_MEM, HEAD_DIM_MEM)

    ns = dec_batch * n_tok
    xs = _ffn(x_sample.reshape(ns, d), n1, w1gu, w1d, nfin, final_norm=False, tm=ns)
    pos_s = jnp.tile(past + jnp.arange(n_tok), dec_batch)
    qa, ka, va, qb, kb, vb, qm_s, gates_s = _proj_sample(xs, nmix, win, _rope_tables(pos_s))

    ck_a, cv_a = _cache_T(cache_k_moba[lyr]), _cache_T(cache_v_moba[lyr])
    ck_b, cv_b = _cache_T(cache_k_sb[lyr]), _cache_T(cache_v_sb[lyr])
    pool = ck_a.shape[0]
    flat = lambda c: c.reshape(pool, W_ATT, PAGE_SIZE)

    idx = _moba_select(page_table, _block_diag_q(qa, n_tok).astype(BF16), flat(ck_a))
    heads = lambda t: t.reshape(dec_batch, n_tok, N_HEADS, HEAD_DIM).transpose(0, 2, 1, 3)
    heads_T = lambda t: _new_tokens_T(t, n_tok).reshape(dec_batch, N_HEADS, HEAD_DIM, PAGE_SIZE)
    oa_s = _moba_gather(page_table.reshape(-1), idx[:, :, :MOBA_TOPK].reshape(-1),
                        heads(qa).astype(BF16), heads_T(ka).astype(BF16), heads_T(va).astype(BF16),
                        ck_a, cv_a, n_pages)
    oa_s = oa_s.transpose(0, 2, 1, 3).reshape(ns, W_ATT)
    ob_s = _sb_decode(page_table, _block_diag_q(qb, n_tok).astype(BF16),
                      _new_tokens_T(kb, n_tok).astype(BF16), _new_tokens_T(vb, n_tok).astype(BF16),
                      flat(ck_b), flat(cv_b), n_tok).reshape(ns, W_ATT)
    om_s = _mem_attend(qm_s.reshape(dec_batch, n_tok, W_MEM),
                       cache_mem_k[lyr].reshape(dec_batch, -1, W_MEM),
                       cache_mem_v[lyr].reshape(dec_batch, -1, W_MEM), tq=n_tok).reshape(ns, W_MEM)
    xs = _merge(xs, oa_s, ob_s, om_s, gates_s, wa, wb, wm, wo, tm=ns)
    y_sample = _ffn(xs, n2, w2gu, w2d, nfin, final_norm=True, tm=ns).reshape(dec_batch, n_tok, d)
    new_rows = lambda t: t.reshape(1, dec_batch, n_tok, N_HEADS, HEAD_DIM)

    return (y_prompt, y_sample, rows_out(kaT), rows_out(vaT), rows_out(kbT), rows_out(vbT),
            mem_out(mk), mem_out(mv), new_rows(ka), new_rows(va), new_rows(kb), new_rows(vb))
```

```python
import functools

import jax
import jax.numpy as jnp
from jax import lax
from jax.experimental import pallas as pl
from jax.experimental.pallas import tpu as pltpu

F32 = jnp.float32
BF16 = jnp.bfloat16

D_MODEL = 1024
HEAD_DIM = 64
N_HEADS = 8
W_ATT = N_HEADS * HEAD_DIM
N_HEADS_MEM = 4
HEAD_DIM_MEM = 128
W_MEM = N_HEADS_MEM * HEAD_DIM_MEM
N_BRANCH = 3
D_IN = 6 * W_ATT + W_MEM + N_BRANCH * D_MODEL
D_FF = 2816
MOBA_BLOCK = 256
MOBA_TOPK = 3
PAGE_SIZE = 128
ROT_DIM = HEAD_DIM // 4
ROPE_THETA = 500000.0
EPS = 1e-6

NEG = -1e30
LOG_F32_TINY = -88.0
LOG2_E = 1.4426950408889634
FF_CHUNK = 256
TOKEN_TILE = 256
LANES = 128
VMEM_LIMIT_BYTES = 56 * 1024 * 1024

_NT = (((1,), (1,)), ((), ()))


def _cparams(sem):
    return pltpu.CompilerParams(dimension_semantics=sem, vmem_limit_bytes=VMEM_LIMIT_BYTES)


def _rms(x, g):
    return x * lax.rsqrt(jnp.mean(x * x, axis=-1, keepdims=True) + EPS) * g


def _softplus_pair(z):
    t = jnp.log(1.0 + jnp.exp2(jnp.abs(z) * -LOG2_E))
    return jnp.maximum(z, 0.0) + t, jnp.minimum(z, 0.0) - t


def _log_sigmoid_pair(z):
    sp, lsig = _softplus_pair(z)
    return -sp, lsig


def _ffn_body(x_ref, g_ref, wgu_ref, wd_ref, gf_ref, o_ref, act_ref, *, final_norm):
    h = _rms(x_ref[...], g_ref[...]).astype(BF16)
    for c in range(D_FF // FF_CHUNK):
        cols = slice(c * FF_CHUNK, (c + 1) * FF_CHUNK)
        up_cols = slice(D_FF + c * FF_CHUNK, D_FF + (c + 1) * FF_CHUNK)
        gate = jnp.dot(h, wgu_ref[:, cols], preferred_element_type=F32)
        up = jnp.dot(h, wgu_ref[:, up_cols], preferred_element_type=F32)
        act_ref[:, cols] = (gate * jax.nn.sigmoid(gate) * up).astype(BF16)
    y = x_ref[...] + 0.5 * jnp.dot(act_ref[...], wd_ref[...], preferred_element_type=F32)
    if final_norm:
        y = _rms(y, gf_ref[...])
    o_ref[...] = y


def _ffn(x, g, w_gu, w_down, g_final, *, final_norm, tm):
    n, d = x.shape
    resident = lambda a: pl.BlockSpec(a.shape, lambda i: (0, 0), pipeline_mode=pl.Buffered(1))
    return pl.pallas_call(
        functools.partial(_ffn_body, final_norm=final_norm),
        out_shape=jax.ShapeDtypeStruct((n, d), F32),
        grid=(n // tm,),
        in_specs=[
            pl.BlockSpec((tm, d), lambda i: (i, 0)),
            pl.BlockSpec((1, d), lambda i: (0, 0)),
            resident(w_gu),
            resident(w_down),
            pl.BlockSpec((1, d), lambda i: (0, 0)),
        ],
        out_specs=pl.BlockSpec((tm, d), lambda i: (i, 0)),
        scratch_shapes=[pltpu.VMEM((tm, D_FF), BF16)],
        compiler_params=_cparams(("parallel",)),
        name="ffn",
    )(x, g, w_gu, w_down, g_final)


def _rope_tables(pos):
    half = ROT_DIM // 2
    inv = ROPE_THETA ** (-(jnp.arange(half, dtype=F32) * 2.0) / ROT_DIM)
    ang = pos.astype(F32)[:, None] * inv[None, :]
    cos, sin = jnp.cos(ang), jnp.sin(ang)
    t = pos.shape[0]
    rest = HEAD_DIM - ROT_DIM
    z8 = jnp.zeros((t, half), F32)
    cos_h = jnp.concatenate([cos, cos, jnp.ones((t, rest), F32)], axis=1)
    sa_h = jnp.concatenate([-sin, z8, jnp.zeros((t, rest), F32)], axis=1)
    sb_h = jnp.concatenate([z8, sin, jnp.zeros((t, rest), F32)], axis=1)
    two = lambda a: jnp.concatenate([a, a], axis=1)
    return two(cos_h), two(sa_h), two(sb_h)


def _rope(z, cos, sa, sb):
    half = ROT_DIM // 2
    w = z.shape[1]
    return z * cos + pltpu.roll(z, w - half, 1) * sa + pltpu.roll(z, half, 1) * sb


def _proj_common(x_ref, g_ref, w_ref, cos_ref, sa_ref, sb_ref):
    h = _rms(x_ref[...], g_ref[...]).astype(BF16)
    reps = W_ATT // LANES
    wide = lambda r: jnp.concatenate([r[...]] * reps, axis=1)
    cos, sa, sb = wide(cos_ref), wide(sa_ref), wide(sb_ref)

    def col(c):
        return jnp.dot(h, w_ref[:, c * W_ATT:(c + 1) * W_ATT], preferred_element_type=F32)

    rope = lambda z: _rope(z, cos, sa, sb)
    return col, rope


def _store_gates(col, gt_ref):
    for c in range(N_BRANCH * D_MODEL // W_ATT):
        gt_ref[:, c * W_ATT:(c + 1) * W_ATT] = jax.nn.sigmoid(col(7 + c)).astype(BF16)


def _proj_prompt_body(x_ref, g_ref, w_ref, cos_ref, sa_ref, sb_ref,
                      qaT_ref, kaT_ref, vaT_ref, kan_ref, vaTb_ref, kbar_ref,
                      qbT_ref, kbT_ref, vbT_ref, kbn_ref, vbTb_ref, qm_ref, gt_ref):
    col, rope = _proj_common(x_ref, g_ref, w_ref, cos_ref, sa_ref, sb_ref)
    qaT_ref[...] = rope(col(0)).T.astype(BF16)
    ka = rope(col(1))
    kaT_ref[...] = ka.T
    kan_ref[...] = ka.astype(BF16)
    kbar_ref[...] = jnp.sum(ka, axis=0, keepdims=True) * (1.0 / MOBA_BLOCK)
    vaT = col(2).T
    vaT_ref[...] = vaT
    vaTb_ref[...] = vaT.astype(BF16)
    qbT_ref[...] = col(3).T.astype(BF16)
    kb = col(4)
    kbT_ref[...] = kb.T
    kbn_ref[...] = kb.astype(BF16)
    vbT = col(5).T
    vbT_ref[...] = vbT
    vbTb_ref[...] = vbT.astype(BF16)
    qm_ref[...] = col(6).astype(BF16)
    _store_gates(col, gt_ref)


def _proj_sample_body(x_ref, g_ref, w_ref, cos_ref, sa_ref, sb_ref,
                      qa_ref, ka_ref, va_ref, qb_ref, kb_ref, vb_ref, qm_ref, gt_ref):
    col, rope = _proj_common(x_ref, g_ref, w_ref, cos_ref, sa_ref, sb_ref)
    qa_ref[...] = rope(col(0))
    ka_ref[...] = rope(col(1))
    va_ref[...] = col(2)
    qb_ref[...] = col(3)
    kb_ref[...] = col(4)
    vb_ref[...] = col(5)
    qm_ref[...] = col(6).astype(BF16)
    _store_gates(col, gt_ref)


def _proj_in_specs(tm, n_tab_tiles):
    tab = pl.BlockSpec((tm, LANES), lambda i: (i % n_tab_tiles, 0))
    return [
        pl.BlockSpec((tm, D_MODEL), lambda i: (i, 0)),
        pl.BlockSpec((1, D_MODEL), lambda i: (0, 0)),
        pl.BlockSpec((D_MODEL, D_IN), lambda i: (0, 0), pipeline_mode=pl.Buffered(1)),
        tab, tab, tab,
    ]


def _proj_prompt(x, g, w_in, tables, batch, seq):
    tm = TOKEN_TILE
    nb = seq // tm
    n = batch * seq
    sds = jax.ShapeDtypeStruct
    t_spec = pl.BlockSpec((None, W_ATT, tm), lambda i: (i // nb, 0, i % nb))
    blk_spec = pl.BlockSpec((None, None, W_ATT, tm), lambda i: (i // nb, i % nb, 0, 0))
    nat_spec = pl.BlockSpec((tm, W_ATT), lambda i: (i, 0))
    t_bf = sds((batch, W_ATT, seq), BF16)
    t_f32 = sds((batch, W_ATT, seq), F32)
    blk_bf = sds((batch, nb, W_ATT, tm), BF16)
    nat_bf = sds((n, W_ATT), BF16)
    out_shape = [t_bf, t_f32, t_f32, nat_bf, blk_bf, sds((n // tm, 1, W_ATT), F32),
                 t_bf, t_f32, t_f32, nat_bf, blk_bf, nat_bf, sds((n, N_BRANCH * D_MODEL), BF16)]
    out_specs = [t_spec, t_spec, t_spec, nat_spec, blk_spec,
                 pl.BlockSpec((None, 1, W_ATT), lambda i: (i, 0, 0)),
                 t_spec, t_spec, t_spec, nat_spec, blk_spec, nat_spec,
                 pl.BlockSpec((tm, N_BRANCH * D_MODEL), lambda i: (i, 0))]
    return pl.pallas_call(
        _proj_prompt_body,
        out_shape=out_shape,
        grid=(n // tm,),
        in_specs=_proj_in_specs(tm, nb),
        out_specs=out_specs,
        compiler_params=_cparams(("parallel",)),
        name="proj_prompt",
    )(x, g, w_in, *tables)


def _proj_sample(x, g, w_in, tables):
    n = x.shape[0]
    sds = jax.ShapeDtypeStruct
    nat = lambda dt: sds((n, W_ATT), dt)
    spec = pl.BlockSpec((n, W_ATT), lambda i: (0, 0))
    out_shape = [nat(F32)] * 6 + [nat(BF16), sds((n, N_BRANCH * D_MODEL), BF16)]
    out_specs = [spec] * 7 + [pl.BlockSpec((n, N_BRANCH * D_MODEL), lambda i: (0, 0))]
    return pl.pallas_call(
        _proj_sample_body,
        out_shape=out_shape,
        grid=(1,),
        in_specs=_proj_in_specs(n, 1),
        out_specs=out_specs,
        compiler_params=_cparams(("arbitrary",)),
        name="proj_sample",
    )(x, g, w_in, *tables)


def _normproj_body(x_ref, g_ref, w_ref, o_ref):
    h = _rms(x_ref[...], g_ref[...]).astype(BF16)
    o_ref[...] = jnp.dot(h, w_ref[...], preferred_element_type=F32)


def _normproj(x, g, w, tm):
    n, d = x.shape
    dout = w.shape[1]
    return pl.pallas_call(
        _normproj_body,
        out_shape=jax.ShapeDtypeStruct((n, dout), F32),
        grid=(n // tm,),
        in_specs=[pl.BlockSpec((tm, d), lambda i: (i, 0)),
                  pl.BlockSpec((1, d), lambda i: (0, 0)),
                  pl.BlockSpec((d, dout), lambda i: (0, 0))],
        out_specs=pl.BlockSpec((tm, dout), lambda i: (i, 0)),
        compiler_params=_cparams(("parallel",)),
        name="mem_kv",
    )(x, g, w)


def _head_masked(qT):
    row = lax.broadcasted_iota(jnp.int32, qT.shape, 0)
    zero = jnp.zeros_like(qT)
    return [jnp.where(row < HEAD_DIM, qT, zero), jnp.where(row >= HEAD_DIM, qT, zero)]


MOBA_SUM_ROWS = 16
MOBA_PAIRS_PER_STEP = 2
SB_PAIRS_PER_STEP = 2


def _moba_prompt_body(qT_ref, k_ref, vT_ref, kbar_ref, oT_ref, sel_ref, m_ref, pv_ref,
                      sa_ref, sb_ref, pa_ref, pb_ref, *, nb):
    qi = pl.program_id(2)
    tq = qT_ref.shape[1]
    pw = 2 * HEAD_DIM
    chains = range(MOBA_PAIRS_PER_STEP)
    pair_rows = lambda c: slice(c * pw, (c + 1) * pw)
    nidx = lax.broadcasted_iota(jnp.int32, (nb, 2 * tq), 0)
    qs = []
    for c in chains:
        q2 = jnp.concatenate(_head_masked(qT_ref[pair_rows(c), :]), axis=1)
        qs.append(q2 * jnp.asarray(HEAD_DIM ** -0.5, BF16))
        sc = jnp.dot(kbar_ref[:, pair_rows(c)].astype(BF16), q2, preferred_element_type=F32)
        sc = jnp.where(nidx < qi, sc, -jnp.inf)
        rank = jnp.zeros((nb, 2 * tq), F32)
        for m in range(nb):
            r = sc[m:m + 1, :]
            beats = (r > sc) | ((r == sc) & (m < nidx))
            rank = rank + jnp.where(beats, 1.0, 0.0)
        sel_ref[c] = jnp.where((nidx < qi) & (rank < MOBA_TOPK), 0.0, NEG)

    def key_block(c, n):
        return k_ref[pl.ds(pl.multiple_of(n * MOBA_BLOCK, MOBA_BLOCK), MOBA_BLOCK), pair_rows(c)]

    def past_logits(c, e):
        n = e - 1
        return jnp.dot(key_block(c, n), qs[c], preferred_element_type=F32) + sel_ref[c, pl.ds(n, 1), :]

    def weights_into(c, e, s_ref, p_ref):
        s = s_ref[c]
        m = jnp.max(s, axis=0, keepdims=True)
        m_ref[c, pl.ds(e, 1), :] = m
        p_ref[c] = jnp.exp((s - m).astype(BF16))

    ones_rows = jnp.ones((MOBA_SUM_ROWS, MOBA_BLOCK), BF16)

    def pv_into(c, slot, p_ref, v_idx):
        pb = p_ref[c]
        vn = vT_ref[v_idx, pair_rows(c), :]
        lhs = [jnp.concatenate([vn[hh * HEAD_DIM:(hh + 1) * HEAD_DIM, :], ones_rows], axis=0) for hh in range(2)]
        pv_ref[c, slot] = jnp.concatenate(
            [jnp.dot(lhs[0], pb[:, :tq], preferred_element_type=F32),
             jnp.dot(lhs[1], pb[:, tq:], preferred_element_type=F32)], axis=1)

    def value_block(e):
        return jnp.where(e == 0, qi, e - 1)

    kpos = lax.broadcasted_iota(jnp.int32, (MOBA_BLOCK, 2 * tq), 0)
    qpos = lax.broadcasted_iota(jnp.int32, (MOBA_BLOCK, 2 * tq), 1) % tq
    causal = jnp.where(kpos <= qpos, 0.0, NEG)
    for c in chains:
        sa_ref[c] = jnp.dot(key_block(c, qi), qs[c], preferred_element_type=F32) + causal
    pb_ref[...] = jnp.zeros_like(pb_ref)
    m_ref[...] = jnp.full(m_ref.shape, NEG, F32)
    spare = nb

    def pair(g, carry):
        e = 2 * g
        for c in chains:
            pv_into(c, jnp.where(e == 0, spare, e - 1), pb_ref, jnp.where(e == 0, 0, value_block(e - 1)))
        for c in chains:
            sb_ref[c] = past_logits(c, e + 1)
        for c in chains:
            weights_into(c, e, sa_ref, pa_ref)
        for c in chains:
            pv_into(c, e, pa_ref, value_block(e))
        for c in chains:
            sa_ref[c] = past_logits(c, e + 2)
        for c in chains:
            weights_into(c, e + 1, sb_ref, pb_ref)
        return carry

    n_pairs = qi // 2 + 1
    lax.fori_loop(0, n_pairs, pair, 0)
    n_elem = 2 * n_pairs
    for c in chains:
        pv_into(c, n_elem - 1, pb_ref, n_elem - 2)

    m_all = [jnp.max(m_ref[c], axis=0, keepdims=True) for c in chains]

    keep = HEAD_DIM + 8
    for c in chains:
        def merge(g, acc, c=c):
            for e in (2 * g, 2 * g + 1):
                acc = acc + jnp.exp(m_ref[c, pl.ds(e, 1), :] - m_all[c]) * pv_ref[c, e, :keep, :]
            return acc

        acc = lax.fori_loop(0, n_pairs, merge, jnp.zeros((keep, 2 * tq), F32))
        o = acc[:HEAD_DIM, :] / acc[HEAD_DIM:HEAD_DIM + 1, :]
        oT_ref[pair_rows(c), :] = jnp.concatenate([o[:, :tq], o[:, tq:]], axis=0).astype(BF16)


def _sb_prompt_body(qT_ref, k_ref, vT_ref, oT_ref):
    qi = pl.program_id(2)
    tq = qT_ref.shape[1]
    pw = 2 * HEAD_DIM
    chains = range(SB_PAIRS_PER_STEP)
    pair_rows = lambda c: slice(c * pw, (c + 1) * pw)
    qs = [jnp.concatenate(_head_masked(qT_ref[pair_rows(c), :]), axis=1) * jnp.asarray(HEAD_DIM ** -0.5, BF16)
          for c in chains]
    kk = lax.broadcasted_iota(jnp.int32, (MOBA_BLOCK, MOBA_BLOCK), 0)
    jj = lax.broadcasted_iota(jnp.int32, (MOBA_BLOCK, MOBA_BLOCK), 1)
    neg_later = jnp.where(kk < jj, -1.0, 0.0).astype(BF16)
    kpos = lax.broadcasted_iota(jnp.int32, (MOBA_BLOCK, 2 * tq), 0)
    qpos = lax.broadcasted_iota(jnp.int32, (MOBA_BLOCK, 2 * tq), 1) % tq

    def blocks_terms(blocks):
        zs = [jnp.dot(k_ref[pl.ds(pl.multiple_of(n * MOBA_BLOCK, MOBA_BLOCK), MOBA_BLOCK), pair_rows(c)], qs[c],
                      preferred_element_type=F32) for c, n, _ in blocks]
        pairs = [_softplus_pair(z) for z in zs]
        sps = [sp if vis is None else jnp.where(vis, sp, 0.0) for (sp, _), (_, _, vis) in zip(pairs, blocks)]
        rs = [jnp.dot(neg_later, sp.astype(BF16), preferred_element_type=F32) for sp in sps]
        out = []
        for (c, n, vis), (_, lsig), sp, r in zip(blocks, pairs, sps, rs):
            a = jnp.exp(lsig + r)
            if vis is not None:
                a = jnp.where(vis, a, 0.0)
            ab = a.astype(BF16)
            vn = vT_ref[n, pair_rows(c), :]
            pv = jnp.concatenate(
                [jnp.dot(vn[:HEAD_DIM, :], ab[:, :tq], preferred_element_type=F32),
                 jnp.dot(vn[HEAD_DIM:, :], ab[:, tq:], preferred_element_type=F32)], axis=1)
            out.append((pv, r[0:1, :] - sp[0:1, :]))
        return out

    strict = kpos < qpos
    prev = jnp.maximum(qi - 1, 0)
    terms = blocks_terms([(c, qi, strict) for c in chains] + [(c, prev, None) for c in chains])
    has_prev = qi > 0
    n_ch = len(chains)
    state = []
    for c in chains:
        (pv_d, tot_d), (pv_p, tot_p) = terms[c], terms[n_ch + c]
        state.append(tot_d + jnp.where(has_prev, tot_p, 0.0))
        state.append(pv_d + jnp.where(has_prev, jnp.exp(tot_d), 0.0) * pv_p)

    def alive_flag(st):
        top = st[0]
        for c in chains[1:]:
            top = jnp.maximum(top, st[2 * c])
        return (jnp.max(top) > LOG_F32_TINY).astype(jnp.int32)

    def cond(carry):
        return jnp.logical_and(carry[0] >= 0, carry[1] > 0)

    def body(carry):
        n, _, st = carry
        terms = blocks_terms([(c, n, None) for c in chains])
        new = []
        for c in chains:
            pv, tot = terms[c]
            new.append(st[2 * c] + tot)
            new.append(st[2 * c + 1] + jnp.exp(st[2 * c]) * pv)
        return n - 1, alive_flag(new), tuple(new)

    fin = lax.while_loop(cond, body, (qi - 2, alive_flag(state), tuple(state)))[2]
    for c in chains:
        o = fin[2 * c + 1]
        oT_ref[pair_rows(c), :] = jnp.concatenate([o[:, :tq], o[:, tq:]], axis=0).astype(BF16)


def _prompt_mixer_specs(seq, nb, pairs):
    tq = TOKEN_TILE
    w = pairs * 2 * HEAD_DIM
    q_spec = pl.BlockSpec((None, w, tq), lambda b, hp, qi: (b, hp, qi))
    k_spec = pl.BlockSpec((None, seq, w), lambda b, hp, qi: (b, 0, hp))
    v_spec = pl.BlockSpec((None, nb, w, MOBA_BLOCK), lambda b, hp, qi: (b, 0, hp, 0))
    return q_spec, k_spec, v_spec


def _moba_prompt(qT, k_nat, vT_blk, kbar):
    batch, _, seq = qT.shape
    nb = seq // MOBA_BLOCK
    pairs = MOBA_PAIRS_PER_STEP
    q_spec, k_spec, v_spec = _prompt_mixer_specs(seq, nb, pairs)
    cols = 2 * TOKEN_TILE
    return pl.pallas_call(
        functools.partial(_moba_prompt_body, nb=nb),
        out_shape=jax.ShapeDtypeStruct(qT.shape, BF16),
        grid=(batch, N_HEADS // (2 * pairs), seq // TOKEN_TILE),
        in_specs=[q_spec, k_spec, v_spec,
                  pl.BlockSpec((None, nb, pairs * 2 * HEAD_DIM), lambda b, hp, qi: (b, 0, hp))],
        out_specs=q_spec,
        scratch_shapes=[pltpu.VMEM((pairs, nb, cols), F32)] * 2
        + [pltpu.VMEM((pairs, nb + 1, HEAD_DIM + MOBA_SUM_ROWS, cols), F32)]
        + [pltpu.VMEM((pairs, MOBA_BLOCK, cols), F32)] * 2
        + [pltpu.VMEM((pairs, MOBA_BLOCK, cols), BF16)] * 2,
        compiler_params=_cparams(("parallel", "parallel", "arbitrary")),
        name="moba_prompt",
    )(qT, k_nat, vT_blk, kbar)


def _sb_prompt(qT, k_nat, vT_blk):
    batch, _, seq = qT.shape
    nb = seq // MOBA_BLOCK
    q_spec, k_spec, v_spec = _prompt_mixer_specs(seq, nb, SB_PAIRS_PER_STEP)
    return pl.pallas_call(
        _sb_prompt_body,
        out_shape=jax.ShapeDtypeStruct(qT.shape, BF16),
        grid=(batch, N_HEADS // (2 * SB_PAIRS_PER_STEP), seq // TOKEN_TILE),
        in_specs=[q_spec, k_spec, v_spec],
        out_specs=q_spec,
        compiler_params=_cparams(("parallel", "parallel", "arbitrary")),
        name="sb_prompt",
    )(qT, k_nat, vT_blk)


def _mem_attend_body(q_ref, k_ref, v_ref, o_ref):
    q = q_ref[...]
    k = k_ref[...].astype(BF16)
    v = v_ref[...].astype(BF16)
    scale = HEAD_DIM_MEM ** -0.5
    outs = []
    for h in range(N_HEADS_MEM):
        sl = slice(h * HEAD_DIM_MEM, (h + 1) * HEAD_DIM_MEM)
        s = lax.dot_general(q[:, sl], k[:, sl], _NT, preferred_element_type=F32) * scale
        p = jnp.exp(s - jnp.max(s, axis=1, keepdims=True))
        o = jnp.dot(p.astype(BF16), v[:, sl], preferred_element_type=F32)
        outs.append(o / jnp.sum(p, axis=1, keepdims=True))
    o_ref[...] = jnp.concatenate(outs, axis=1).astype(BF16)


def _mem_attend(q, k, v, tq):
    batch, seq, w = q.shape
    n_mem = k.shape[1]
    kv_spec = pl.BlockSpec((None, n_mem, w), lambda b, i: (b, 0, 0))
    q_spec = pl.BlockSpec((None, tq, w), lambda b, i: (b, i, 0))
    return pl.pallas_call(
        _mem_attend_body,
        out_shape=jax.ShapeDtypeStruct(q.shape, BF16),
        grid=(batch, seq // tq),
        in_specs=[q_spec, kv_spec, kv_spec],
        out_specs=q_spec,
        compiler_params=_cparams(("parallel", "parallel")),
        name="mem_attend",
    )(q, k, v)


def _merge_body(x_ref, oa_ref, ob_ref, om_ref, gt_ref, wa_ref, wb_ref, wm_ref, wo_ref, o_ref):
    merged = None
    for i, (o, w) in enumerate(((oa_ref, wa_ref), (ob_ref, wb_ref), (om_ref, wm_ref))):
        br = jnp.dot(o[...], w[...], preferred_element_type=F32)
        term = gt_ref[:, i * D_MODEL:(i + 1) * D_MODEL].astype(F32) * br
        merged = term if merged is None else merged + term
    o_ref[...] = x_ref[...] + jnp.dot(merged.astype(BF16), wo_ref[...], preferred_element_type=F32)


def _merge(x, oa, ob, om, gates, wa, wb, wm, wo, tm):
    n, d = x.shape
    row = lambda w: pl.BlockSpec((tm, w), lambda i: (i, 0))
    full = lambda a: pl.BlockSpec(a.shape, lambda i: (0, 0))
    return pl.pallas_call(
        _merge_body,
        out_shape=jax.ShapeDtypeStruct((n, d), F32),
        grid=(n // tm,),
        in_specs=[row(d), row(W_ATT), row(W_ATT), row(W_MEM), row(N_BRANCH * d),
                  full(wa), full(wb), full(wm), full(wo)],
        out_specs=row(d),
        compiler_params=_cparams(("parallel",)),
        name="merge",
    )(x, oa, ob, om, gates, wa, wb, wm, wo)


SELECT_PAGES_PER_STEP = 32


def _own_head_columns(acc, n_tok):
    lane_head = lax.broadcasted_iota(jnp.int32, (n_tok, W_ATT), 1) // HEAD_DIM
    out = jnp.zeros((n_tok, W_ATT), F32)
    for h in range(N_HEADS):
        out = out + jnp.where(lane_head == h, acc[h * n_tok:(h + 1) * n_tok, :], 0.0)
    return out


def _suffix_sum_lanes(x):
    lane = lax.broadcasted_iota(jnp.int32, x.shape, 1)
    n = x.shape[1]
    sh = 1
    while sh < n:
        x = x + jnp.where(lane < n - sh, pltpu.roll(x, n - sh, 1), 0.0)
        sh *= 2
    return x


SB_RING = 4


def _sb_decode_body(pt_ref, qbd_ref, knT_ref, vnT_ref, ck_hbm, cv_hbm, o_ref, kbuf, vbuf, sem,
                    *, n_tok, n_pages):
    b = pl.program_id(0)
    qbd = qbd_ref[...] * jnp.asarray(HEAD_DIM ** -0.5, BF16)
    rows = N_HEADS * n_tok
    ahead = SB_RING - 1

    def page_copies(p):
        slot = p % SB_RING
        phys = pt_ref[b, n_pages - 1 - p]
        return (pltpu.make_async_copy(ck_hbm.at[phys], kbuf.at[slot], sem.at[0, slot]),
                pltpu.make_async_copy(cv_hbm.at[phys], vbuf.at[slot], sem.at[1, slot]))

    def start(p):
        for c in page_copies(p):
            c.start()

    def wait(p):
        for c in page_copies(p):
            c.wait()

    def page(kT, vT, lsurv, visible):
        z = jnp.dot(qbd, kT.astype(BF16), preferred_element_type=F32)
        l, ls = _log_sigmoid_pair(z)
        if visible is not None:
            l = jnp.where(visible, l, 0.0)
        suffix = _suffix_sum_lanes(l)
        a = jnp.exp(ls + (suffix - l) + lsurv)
        if visible is not None:
            a = jnp.where(visible, a, 0.0)
        o = lax.dot_general(a.astype(BF16), vT.astype(BF16), _NT, preferred_element_type=F32)
        return o, lsurv + suffix[:, 0:1]

    def alive_flag(lsurv):
        return (jnp.max(lsurv) > LOG_F32_TINY).astype(jnp.int32)

    for p in range(ahead):
        start(p)

    key = lax.broadcasted_iota(jnp.int32, (rows, PAGE_SIZE), 1)
    tok = lax.broadcasted_iota(jnp.int32, (rows, PAGE_SIZE), 0) % n_tok
    acc0, ls0 = page(knT_ref[...], vnT_ref[...], jnp.zeros((rows, 1), F32), key < tok)

    def cond(c):
        return jnp.logical_and(c[0] < n_pages, c[3] > 0)

    def body(c):
        p, lsurv, acc, _ = c
        wait(p)

        @pl.when(p + ahead < n_pages)
        def _():
            start(p + ahead)

        slot = p % SB_RING
        o, lsurv = page(kbuf[slot], vbuf[slot], lsurv, None)
        return p + 1, lsurv, acc + o, alive_flag(lsurv)

    p_end, _, acc, _ = lax.while_loop(cond, body, (0, ls0, acc0, alive_flag(ls0)))

    for j in range(ahead):
        @pl.when(p_end + j < n_pages)
        def _():
            wait(p_end + j)

    o_ref[...] = _own_head_columns(acc, n_tok).astype(BF16)


def _sb_decode(page_table, qbd, knT, vnT, ckT, cvT, n_tok):
    batch, n_pages = page_table.shape
    assert n_pages >= SB_RING
    rows = N_HEADS * n_tok
    per_b = lambda r, c: pl.BlockSpec((None, r, c), lambda b, pt: (b, 0, 0))
    hbm = pl.BlockSpec(memory_space=pl.ANY)
    return pl.pallas_call(
        functools.partial(_sb_decode_body, n_tok=n_tok, n_pages=n_pages),
        out_shape=jax.ShapeDtypeStruct((batch, n_tok, W_ATT), BF16),
        grid_spec=pltpu.PrefetchScalarGridSpec(
            num_scalar_prefetch=1,
            grid=(batch,),
            in_specs=[per_b(rows, W_ATT), per_b(W_ATT, PAGE_SIZE), per_b(W_ATT, PAGE_SIZE), hbm, hbm],
            out_specs=per_b(n_tok, W_ATT),
            scratch_shapes=[pltpu.VMEM((SB_RING, W_ATT, PAGE_SIZE), F32),
                            pltpu.VMEM((SB_RING, W_ATT, PAGE_SIZE), F32),
                            pltpu.SemaphoreType.DMA((2, SB_RING))],
        ),
        compiler_params=_cparams(("arbitrary",)),
        name="sb_decode",
    )(page_table, qbd, knT, vnT, ckT, cvT)


def _moba_select_body(pt_ref, qbd_ref, *refs, n_pages):
    g = SELECT_PAGES_PER_STEP
    k_refs = refs[:g]
    idx_ref, sc_ref = refs[g:]
    step = pl.program_id(1)
    ppb = MOBA_BLOCK // PAGE_SIZE
    qbd = qbd_ref[...]
    rows = qbd.shape[0]
    lane = lax.broadcasted_iota(jnp.int32, (rows, LANES), 1)

    @pl.when(step == 0)
    def _():
        sc_ref[...] = jnp.zeros_like(sc_ref)

    sc = sc_ref[...]
    for i in range(0, g, ppb):
        ksum = k_refs[i][...]
        for j in range(1, ppb):
            ksum = ksum + k_refs[i + j][...]
        z = jnp.dot(qbd, ksum.astype(BF16), preferred_element_type=F32)
        blk = (step * g + i) // ppb
        sc = sc + jnp.where(lane == blk, jnp.sum(z, axis=1, keepdims=True), 0.0)
    sc_ref[...] = sc

    @pl.when(step == n_pages // g - 1)
    def _():
        nb = n_pages // ppb
        s = jnp.where(lane < nb, sc_ref[...], -jnp.inf)
        out = jnp.zeros((rows, LANES), jnp.int32)
        for j in range(MOBA_TOPK):
            best = jnp.max(s, axis=1, keepdims=True)
            pick = jnp.min(jnp.where(s == best, lane, LANES), axis=1, keepdims=True)
            out = jnp.where(lane == j, pick, out)
            s = jnp.where(lane == pick, -jnp.inf, s)
        idx_ref[...] = out


def _moba_select(page_table, qbd, ckT):
    batch, n_pages = page_table.shape
    g = SELECT_PAGES_PER_STEP
    rows = qbd.shape[1]

    def page_spec(i):
        return pl.BlockSpec((None, W_ATT, PAGE_SIZE), lambda b, s, pt: (pt[b, s * g + i], 0, 0))

    return pl.pallas_call(
        functools.partial(_moba_select_body, n_pages=n_pages),
        out_shape=jax.ShapeDtypeStruct((batch, rows, LANES), jnp.int32),
        grid_spec=pltpu.PrefetchScalarGridSpec(
            num_scalar_prefetch=1,
            grid=(batch, n_pages // g),
            in_specs=[pl.BlockSpec((None, rows, W_ATT), lambda b, s, pt: (b, 0, 0))]
            + [page_spec(i) for i in range(g)],
            out_specs=pl.BlockSpec((None, rows, LANES), lambda b, s, pt: (b, 0, 0)),
            scratch_shapes=[pltpu.VMEM((rows, LANES), F32)],
        ),
        compiler_params=_cparams(("parallel", "arbitrary")),
        name="moba_select",
    )(page_table, qbd, *([ckT] * g))


def _moba_gather_body(pt_ref, idx_ref, q_ref, knT_ref, vnT_ref, ck_hbm, cv_hbm, o_ref,
                      kbuf, vbuf, sem, *, n_tok, n_pages):
    ppb = MOBA_BLOCK // PAGE_SIZE
    n_slots = n_tok * MOBA_TOPK
    scale = HEAD_DIM ** -0.5
    step = pl.program_id(0) * N_HEADS + pl.program_id(1)
    n_steps = pl.num_programs(0) * N_HEADS
    half = step % 2

    def start_copies(st, hf):
        bb, hh = st // N_HEADS, st % N_HEADS
        for slot in range(n_slots):
            tok, j = divmod(slot, MOBA_TOPK)
            blk = idx_ref[((bb * N_HEADS + hh) * n_tok + tok) * MOBA_TOPK + j]
            for i in range(ppb):
                pg = pt_ref[bb * n_pages + blk * ppb + i]
                dst = pl.ds(slot * MOBA_BLOCK + i * PAGE_SIZE, PAGE_SIZE)
                pltpu.make_async_copy(ck_hbm.at[pg, hh], kbuf.at[hf, :, dst], sem.at[0, hf]).start()
                pltpu.make_async_copy(cv_hbm.at[pg, hh], vbuf.at[hf, :, dst], sem.at[1, hf]).start()

    @pl.when(step == 0)
    def _():
        start_copies(step, half)

    @pl.when(step + 1 < n_steps)
    def _():
        start_copies(step + 1, 1 - half)

    kbuf = kbuf.at[half]
    vbuf = vbuf.at[half]
    pltpu.make_async_copy(kbuf, kbuf, sem.at[0, half]).wait()
    pltpu.make_async_copy(vbuf, vbuf, sem.at[1, half]).wait()
    q = q_ref[...] * jnp.asarray(scale, BF16)
    key = lax.broadcasted_iota(jnp.int32, (n_tok, PAGE_SIZE), 1)
    tok = lax.broadcasted_iota(jnp.int32, (n_tok, PAGE_SIZE), 0)
    s_own = jnp.dot(q, knT_ref[...], preferred_element_type=F32) + jnp.where(key <= tok, 0.0, NEG)
    width = n_slots * MOBA_BLOCK
    col_tok = lax.broadcasted_iota(jnp.int32, (n_tok, width), 1) // (MOBA_TOPK * MOBA_BLOCK)
    row_tok = lax.broadcasted_iota(jnp.int32, (n_tok, width), 0)
    s_sel = (jnp.dot(q, kbuf[...].astype(BF16), preferred_element_type=F32)
             + jnp.where(col_tok == row_tok, 0.0, NEG))
    m = jnp.maximum(jnp.max(s_own, axis=1, keepdims=True), jnp.max(s_sel, axis=1, keepdims=True))
    p_own = jnp.exp(s_own - m)
    p_sel = jnp.exp(s_sel - m)
    den = jnp.sum(p_own, axis=1, keepdims=True) + jnp.sum(p_sel, axis=1, keepdims=True)
    acc = (lax.dot_general(p_own.astype(BF16), vnT_ref[...], _NT, preferred_element_type=F32)
           + lax.dot_general(p_sel.astype(BF16), vbuf[...].astype(BF16), _NT, preferred_element_type=F32))
    o_ref[...] = (acc / den).astype(BF16)


def _moba_gather(pt_flat, idx_flat, q, knT, vnT, ckT, cvT, n_pages):
    batch, _, n_tok, _ = q.shape
    n_slots = n_tok * MOBA_TOPK
    per_bh = lambda r, c: pl.BlockSpec((None, None, r, c), lambda b, h, pt, ix: (b, h, 0, 0))
    return pl.pallas_call(
        functools.partial(_moba_gather_body, n_tok=n_tok, n_pages=n_pages),
        out_shape=jax.ShapeDtypeStruct((batch, N_HEADS, n_tok, HEAD_DIM), BF16),
        grid_spec=pltpu.PrefetchScalarGridSpec(
            num_scalar_prefetch=2,
            grid=(batch, N_HEADS),
            in_specs=[per_bh(n_tok, HEAD_DIM), per_bh(HEAD_DIM, PAGE_SIZE), per_bh(HEAD_DIM, PAGE_SIZE),
                      pl.BlockSpec(memory_space=pl.ANY), pl.BlockSpec(memory_space=pl.ANY)],
            out_specs=per_bh(n_tok, HEAD_DIM),
            scratch_shapes=[pltpu.VMEM((2, HEAD_DIM, n_slots * MOBA_BLOCK), F32),
                            pltpu.VMEM((2, HEAD_DIM, n_slots * MOBA_BLOCK), F32),
                            pltpu.SemaphoreType.DMA((2, 2))],
        ),
        compiler_params=_cparams(("arbitrary", "arbitrary")),
        name="moba_gather",
    )(pt_flat, idx_flat, q, knT, vnT, ckT, cvT)


def _block_diag_q(q, n_tok):
    batch = q.shape[0] // n_tok
    q3 = q.reshape(batch, 1, n_tok, W_ATT)
    col_head = (jnp.arange(W_ATT) // HEAD_DIM)[None, None, None, :]
    row_head = jnp.arange(N_HEADS)[None, :, None, None]
    return jnp.where(col_head == row_head, q3, 0.0).reshape(batch, N_HEADS * n_tok, W_ATT)


def _new_tokens_T(x, n_tok):
    batch = x.shape[0] // n_tok
    xt = x.reshape(batch, n_tok, W_ATT).transpose(0, 2, 1)
    return jnp.pad(xt, ((0, 0), (0, 0), (0, PAGE_SIZE - n_tok)))


def _cache_T(c):
    return c.transpose(0, 2, 3, 1)


def kernel(x_prompt, x_sample, cache_k_moba, cache_v_moba, cache_k_sb, cache_v_sb, cache_mem_k, cache_mem_v,
           page_table, mem_prompt, norm_ffn1, w_ffn1_gu, w_ffn1_down, norm_mix, w_in, norm_mem, w_mem_kv,
           w_br_moba, w_br_sb, w_br_mem, w_out, norm_ffn2, w_ffn2_gu, w_ffn2_down, norm_final):
    batch, seq, d = x_prompt.shape
    dec_batch, n_tok, _ = x_sample.shape
    n_pages = page_table.shape[1]
    past = n_pages * PAGE_SIZE
    depth = w_in.shape[0]
    assert depth == 1 and past % MOBA_BLOCK == 0 and seq % TOKEN_TILE == 0

    lyr = 0
    bf = lambda w: w[lyr].astype(BF16)
    row = lambda g: g.reshape(1, d)
    w1gu, w1d, w2gu, w2d = bf(w_ffn1_gu), bf(w_ffn1_down), bf(w_ffn2_gu), bf(w_ffn2_down)
    win, wmem, wa, wb, wm, wo = bf(w_in), bf(w_mem_kv), bf(w_br_moba), bf(w_br_sb), bf(w_br_mem), bf(w_out)
    n1, nmix, nmem, n2, nfin = row(norm_ffn1[lyr]), row(norm_mix[lyr]), row(norm_mem[lyr]), row(norm_ffn2[lyr]), row(norm_final)

    n = batch * seq
    nb = seq // MOBA_BLOCK
    xp = _ffn(x_prompt.reshape(n, d), n1, w1gu, w1d, nfin, final_norm=False, tm=1024)
    (qaT, kaT, vaT, ka_nat, vaT_blk, kbar, qbT, kbT, vbT, kb_nat, vbT_blk, qm, gates) = _proj_prompt(
        xp, nmix, win, _rope_tables(jnp.arange(seq)), batch, seq)
    mem_kv = _normproj(mem_prompt.reshape(-1, d), nmem, wmem, tm=512)
    n_mem = mem_prompt.shape[1]
    mk = mem_kv[:, :W_MEM].reshape(batch, n_mem, W_MEM)
    mv = mem_kv[:, W_MEM:].reshape(batch, n_mem, W_MEM)
    oaT = _moba_prompt(qaT, ka_nat.reshape(batch, seq, W_ATT), vaT_blk, kbar.reshape(batch, nb, W_ATT))
    obT = _sb_prompt(qbT, kb_nat.reshape(batch, seq, W_ATT), vbT_blk)
    om = _mem_attend(qm.reshape(batch, seq, W_MEM), mk, mv, tq=512)
    nat = lambda t: t.transpose(0, 2, 1).reshape(n, W_ATT)
    xp = _merge(xp, nat(oaT), nat(obT), om.reshape(n, W_MEM), gates, wa, wb, wm, wo, tm=512)
    y_prompt = _ffn(xp, n2, w2gu, w2d, nfin, final_norm=True, tm=1024).reshape(batch, seq, d)
    rows_out = lambda t: t.reshape(batch, N_HEADS, HEAD_DIM, seq).transpose(0, 3, 1, 2)[None]
    mem_out = lambda t: t.reshape(1, batch, n_mem, N_HEADS_MEM, HEAD_DIM_MEM)

    ns = dec_batch * n_tok
    xs = _ffn(x_sample.reshape(ns, d), n1, w1gu, w1d, nfin, final_norm=False, tm=ns)
    pos_s = jnp.tile(past + jnp.arange(n_tok), dec_batch)
    qa, ka, va, qb, kb, vb, qm_s, gates_s = _proj_sample(xs, nmix, win, _rope_tables(pos_s))

    ck_a, cv_a = _cache_T(cache_k_moba[lyr]), _cache_T(cache_v_moba[lyr])
    ck_b, cv_b = _cache_T(cache_k_sb[lyr]), _cache_T(cache_v_sb[lyr])
    pool = ck_a.shape[0]
    flat = lambda c: c.reshape(pool, W_ATT, PAGE_SIZE)

    idx = _moba_select(page_table, _block_diag_q(qa, n_tok).astype(BF16), flat(ck_a))
    heads = lambda t: t.reshape(dec_batch, n_tok, N_HEADS, HEAD_DIM).transpose(0, 2, 1, 3)
    heads_T = lambda t: _new_tokens_T(t, n_tok).reshape(dec_batch, N_HEADS, HEAD_DIM, PAGE_SIZE)
    oa_s = _moba_gather(page_table.reshape(-1), idx[:, :, :MOBA_TOPK].reshape(-1),
                        heads(qa).astype(BF16), heads_T(ka).astype(BF16), heads_T(va).astype(BF16),
                        ck_a, cv_a, n_pages)
    oa_s = oa_s.transpose(0, 2, 1, 3).reshape(ns, W_ATT)
    ob_s = _sb_decode(page_table, _block_diag_q(qb, n_tok).astype(BF16),
                      _new_tokens_T(kb, n_tok).astype(BF16), _new_tokens_T(vb, n_tok).astype(BF16),
                      flat(ck_b), flat(cv_b), n_tok).reshape(ns, W_ATT)
    om_s = _mem_attend(qm_s.reshape(dec_batch, n_tok, W_MEM),
                       cache_mem_k[lyr].reshape(dec_batch, -1, W_MEM),
                       cache_mem_v[lyr].reshape(dec_batch, -1, W_MEM), tq=n_tok).reshape(ns, W_MEM)
    xs = _merge(xs, oa_s, ob_s, om_s, gates_s, wa, wb, wm, wo, tm=ns)
    y_sample = _ffn(xs, n2, w2gu, w2d, nfin, final_norm=True, tm=ns).reshape(dec_batch, n_tok, d)
    new_rows = lambda t: t.reshape(1, dec_batch, n_tok, N_HEADS, HEAD_DIM)

    return (y_prompt, y_sample, rows_out(kaT), rows_out(vaT), rows_out(kbT), rows_out(vbT),
            mem_out(mk), mem_out(mv), new_rows(ka), new_rows(va), new_rows(kb), new_rows(vb))
```
